```python
import jax
import jax.numpy as jnp
from jax import lax
import numpy as np

D_MODEL = 2048
BATCH = 2
SEQ = 4096
DEPTH = 4
DEC_BATCH = 8
DEC_SEQ = 8
PAST_LEN = 16384
PAGE_SIZE = 128

D_ATTN = D_MODEL // 2
D_RNN = D_MODEL - D_ATTN
N_HEADS = 8
HEAD_DIM = D_ATTN // N_HEADS
KV_HEADS = 2
Q_PER_KV = N_HEADS // KV_HEADS
BLOCK = 64
N_SELECT = 16
WINDOW = 512
Q_BLOCK = 128
FORCED_SCORE = 1e4
KV_SLOTS = 4
CONV_WIDTH = 4
LRU_BLOCKS = 8
LRU_BW = D_RNN // LRU_BLOCKS
LRU_C = 8.0
N_EXPERTS = 16
N_GROUPS = 4
EXPERTS_PER_GROUP = N_EXPERTS // N_GROUPS
GROUP_SCORE_TOPK = 2
TOP_K = 2
D_EXPERT = 512
ALPHA = (2 * DEPTH) ** 0.25
BETA = (8 * DEPTH) ** -0.25
LN_EPS = 1e-5
COL_SIZES = (N_HEADS * HEAD_DIM,) + (KV_HEADS * HEAD_DIM,) * 6 + (3 * N_HEADS, D_RNN, D_RNN)
IN_COLS = sum(COL_SIZES)

kernel_name = 'hymba_nsa_rglru_grouped_moe_step'


def layer_norm(x, g, b):
    xf = x.astype(jnp.float32)
    xc = xf - xf.mean(-1, keepdims=True)
    var = (xc * xc).mean(-1, keepdims=True)
    return (xc * lax.rsqrt(var + LN_EPS) * g + b).astype(x.dtype)


def rms_norm(x, g):
    xf = x.astype(jnp.float32)
    return (xf * lax.rsqrt((xf * xf).mean(-1, keepdims=True) + LN_EPS) * g).astype(x.dtype)


def masked_softmax(s, mask):
    s = jnp.where(mask, s.astype(jnp.float32), -jnp.inf)
    m = jnp.max(s, axis=-1, keepdims=True)
    m = jnp.where(jnp.isfinite(m), m, 0.0)
    e = jnp.where(mask, jnp.exp(s - m), 0.0)
    return e / jnp.maximum(e.sum(-1, keepdims=True), 1e-30)


def in_proj(x, w):
    b, t, _ = x.shape
    proj = jnp.einsum('btd,dc->btc', x, w)
    parts, start = [], 0
    for size in COL_SIZES:
        parts.append(proj[..., start:start + size])
        start += size
    q = parts[0].reshape(b, t, KV_HEADS, Q_PER_KV, HEAD_DIM)
    kvs = [p.reshape(b, t, KV_HEADS, HEAD_DIM) for p in parts[1:7]]
    gates = jax.nn.sigmoid(parts[7].astype(jnp.float32)).reshape(b, t, 3, KV_HEADS, Q_PER_KV).astype(x.dtype)
    return q, kvs, gates, parts[8], parts[9]


def compress(rows, pe, w1, w2):
    b, n, g, d = rows.shape
    nc = n // BLOCK
    blk = rows[:, :nc * BLOCK].reshape(b, nc, BLOCK, g, d) + pe[None, None, :, None, :]
    h = jax.nn.gelu(jnp.einsum('bnlgd,lde->bnge', blk, w1))
    return jnp.einsum('bnge,ef->bngf', h, w2)


def to_blocks(rows):
    b, n, g, d = rows.shape
    ns = -(-n // BLOCK)
    rows = jnp.pad(rows, ((0, 0), (0, ns * BLOCK - n), (0, 0), (0, 0)))
    return rows.reshape(b, ns, BLOCK, g, d).transpose(0, 3, 1, 2, 4)


def nsa_attend(q, gates, qpos, kc, vc, ks, vs, kw, vw, kwpos):
    scale = HEAD_DIM ** -0.5
    b, nq = q.shape[:2]
    nc, ns = kc.shape[1], ks.shape[2]

    c_end = (jnp.arange(nc) + 1) * BLOCK - 1
    c_mask = (c_end[None, :] <= qpos[:, None])[None, :, None, None, :]
    p_cmp = masked_softmax(jnp.einsum('bqgrd,bngd->bqgrn', q, kc) * scale, c_mask)
    o_cmp = jnp.einsum('bqgrn,bngd->bqgrd', p_cmp.astype(vc.dtype), vc)

    imp = jnp.pad(p_cmp.sum(3), ((0, 0), (0, 0), (0, 0), (0, ns - nc)))
    blk = jnp.arange(ns)[None, :]
    cur = (qpos // BLOCK)[:, None]
    forced = (blk == 0) | (blk == cur) | (blk == cur - 1)
    valid = blk <= cur
    score = jnp.where(forced[None, :, None], FORCED_SCORE, imp)
    score = jnp.where(valid[None, :, None], score, -1.0)
    n_sel = min(N_SELECT, ns)
    _, idx = lax.top_k(score, n_sel)
    idx = idx.transpose(0, 2, 1, 3)
    gather = jax.vmap(jax.vmap(lambda blocks, i: blocks[i]))
    k_sel = gather(ks, idx)
    v_sel = gather(vs, idx)
    pos = idx[..., None] * BLOCK + jnp.arange(BLOCK)
    s_mask = (pos <= qpos[None, None, :, None, None]).transpose(0, 2, 1, 3, 4)
    s_mask = s_mask[:, :, :, None].reshape(b, nq, KV_HEADS, 1, n_sel * BLOCK)
    s_sel = jnp.einsum('bqgrd,bgqnkd->bqgrnk', q, k_sel).reshape(b, nq, KV_HEADS, Q_PER_KV, n_sel * BLOCK) * scale
    p_sel = masked_softmax(s_sel, s_mask).reshape(b, nq, KV_HEADS, Q_PER_KV, n_sel, BLOCK)
    o_sel = jnp.einsum('bqgrnk,bgqnkd->bqgrd', p_sel.astype(v_sel.dtype), v_sel)

    w_mask = ((kwpos[None, :] <= qpos[:, None]) & (kwpos[None, :] > qpos[:, None] - WINDOW)
              & (kwpos[None, :] >= 0))[None, :, None, None, :]
    p_win = masked_softmax(jnp.einsum('bqgrd,bkgd->bqgrk', q, kw) * scale, w_mask)
    o_win = jnp.einsum('bqgrk,bkgd->bqgrd', p_win.astype(vw.dtype), vw)

    o = (gates[:, :, 0][..., None] * o_cmp + gates[:, :, 1][..., None] * o_sel
         + gates[:, :, 2][..., None] * o_win)
    return o.reshape(b, nq, N_HEADS * HEAD_DIM).astype(q.dtype)


def prompt_attention(q, gates, kvs, pe, w1, w2):
    k_cmp, v_cmp, k_sel, v_sel, k_win, v_win = kvs
    b, t = q.shape[:2]
    kc, vc = compress(k_cmp, pe[0], w1[0], w2[0]), compress(v_cmp, pe[1], w1[1], w2[1])
    ks, vs = to_blocks(k_sel), to_blocks(v_sel)
    pad = ((0, 0), (WINDOW, 0), (0, 0), (0, 0))
    kw_pad, vw_pad = jnp.pad(k_win, pad), jnp.pad(v_win, pad)
    span = Q_BLOCK + WINDOW

    def block(i):
        s = i * Q_BLOCK
        qb = lax.dynamic_slice_in_dim(q, s, Q_BLOCK, axis=1)
        gb = lax.dynamic_slice_in_dim(gates, s, Q_BLOCK, axis=1)
        kwb = lax.dynamic_slice_in_dim(kw_pad, s, span, axis=1)
        vwb = lax.dynamic_slice_in_dim(vw_pad, s, span, axis=1)
        return nsa_attend(qb, gb, s + jnp.arange(Q_BLOCK), kc, vc, ks, vs, kwb, vwb,
                          s - WINDOW + jnp.arange(span))

    out = lax.map(block, jnp.arange(t // Q_BLOCK))
    return out.transpose(1, 0, 2, 3).reshape(b, t, N_HEADS * HEAD_DIM)


def sample_attention(q, gates, kvs, past_rows, win_buf, pe, w1, w2):
    db, ds = q.shape[:2]
    past_len, wb = past_rows.shape[1], win_buf.shape[1]
    rows = jnp.concatenate([past_rows, jnp.stack(kvs[:4], axis=2).astype(past_rows.dtype)], axis=1)
    win = jnp.concatenate([win_buf, jnp.stack(kvs[4:], axis=2).astype(win_buf.dtype)], axis=1)
    out = nsa_attend(q, gates, past_len + jnp.arange(ds),
                     compress(rows[:, :, 0], pe[0], w1[0], w2[0]),
                     compress(rows[:, :, 1], pe[1], w1[1], w2[1]),
                     to_blocks(rows[:, :, 2]), to_blocks(rows[:, :, 3]),
                     win[:, :, 0], win[:, :, 1], past_len - wb + jnp.arange(wb + ds))
    return out, win[:, -min(WINDOW, past_len + ds):]


def _lin_combine(left, right):
    a_l, b_l = left
    a_r, b_r = right
    return a_l * a_r, a_r * b_l + b_r


def rglru_branch(u, gate, conv_buf, h0, conv_w, conv_b, w_a, b_a, w_i, b_i, lam):
    b, t, c = u.shape
    xp = jnp.concatenate([conv_buf.astype(u.dtype), u], axis=1)
    xc = conv_b + xp[:, 0:t] * conv_w[0]
    for j in range(1, CONV_WIDTH):
        xc = xc + xp[:, j:j + t] * conv_w[j]
    xf = xc.astype(jnp.float32)
    xb = xf.reshape(b, t, LRU_BLOCKS, LRU_BW)
    r = jax.nn.sigmoid(jnp.einsum('btnc,ncd->btnd', xb, w_a).reshape(b, t, c) + b_a)
    i = jax.nn.sigmoid(jnp.einsum('btnc,ncd->btnd', xb, w_i).reshape(b, t, c) + b_i)
    log_a = -LRU_C * jax.nn.softplus(-lam.astype(jnp.float32)) * r
    a = jnp.exp(log_a)
    inp = jnp.sqrt(-jnp.expm1(2.0 * log_a)) * (i * xf)
    inp = inp.at[:, 0].add(a[:, 0] * h0.astype(jnp.float32))
    _, h = lax.associative_scan(_lin_combine, (a, inp), axis=1)
    y = (h * jax.nn.gelu(gate.astype(jnp.float32))).astype(u.dtype)
    return y, xp[:, -(CONV_WIDTH - 1):], h[:, -1]


def mixer_out(attn, rnn, gn, w):
    y = jnp.concatenate([rms_norm(attn, gn[:D_ATTN]), rms_norm(rnn, gn[D_ATTN:])], axis=-1)
    return jnp.einsum('btc,cd->btd', y, w)


def grouped_moe(x, router_w, router_b, w_gate, w_up, w_down):
    shp = x.shape
    xt = x.reshape(-1, shp[-1])
    n = xt.shape[0]
    s = jax.nn.sigmoid(jnp.einsum('nd,de->ne', xt, router_w).astype(jnp.float32))
    biased = s + router_b
    grp_score = lax.top_k(biased.reshape(n, N_GROUPS, EXPERTS_PER_GROUP), GROUP_SCORE_TOPK)[0].sum(-1)
    grp = jnp.argmax(grp_score, axis=-1)
    in_grp = (jnp.arange(N_EXPERTS) // EXPERTS_PER_GROUP)[None, :] == grp[:, None]
    _, eidx = lax.top_k(jnp.where(in_grp, biased, -jnp.inf), TOP_K)
    wsel = jnp.take_along_axis(s, eidx, axis=-1)
    wsel = wsel / wsel.sum(-1, keepdims=True)
    combine = jnp.einsum('nk,nke->ne', wsel, jax.nn.one_hot(eidx, N_EXPERTS, dtype=jnp.float32))
    h = jax.nn.silu(jnp.einsum('nd,edf->nef', xt, w_gate)) * jnp.einsum('nd,edf->nef', xt, w_up)
    y = jnp.einsum('nef,efd->nd', h * combine[..., None].astype(h.dtype), w_down)
    return y.reshape(shp).astype(x.dtype)


def setup_inputs(seed: int = 0) -> dict:
    key = jax.random.key(seed)
    ks = jax.random.split(key, 32)
    f32 = jnp.float32

    def nrm(k, shape, s):
        return jax.random.normal(k, shape, f32) * s

    n_pages = PAST_LEN // PAGE_SIZE
    n_pool = (DEC_BATCH * n_pages * 5) // 4
    win_len = min(WINDOW, PAST_LEN)
    col_scale = np.ones((IN_COLS,), np.float32)
    start = 0
    for ci, size in enumerate(COL_SIZES):
        if ci in (2, 4, 6):
            col_scale[start:start + size] = BETA
        start += size
    u = jax.random.uniform(ks[17], (DEPTH, D_RNN), f32, 0.9, 0.999)
    a0 = u ** (1.0 / LRU_C)
    return {
        'x_prompt': nrm(ks[0], (BATCH, SEQ, D_MODEL), 1.0),
        'x_sample': nrm(ks[1], (DEC_BATCH, DEC_SEQ, D_MODEL), 1.0),
        'cache_kv': nrm(ks[2], (DEPTH, n_pool, PAGE_SIZE, KV_SLOTS, KV_HEADS, HEAD_DIM), 1.0),
        'cache_win': nrm(ks[3], (DEPTH, DEC_BATCH, win_len, 2, KV_HEADS, HEAD_DIM), 1.0),
        'state_conv': nrm(ks[4], (DEPTH, DEC_BATCH, CONV_WIDTH - 1, D_RNN), 1.0),
        'state_h': nrm(ks[5], (DEPTH, DEC_BATCH, D_RNN), 0.5),
        'page_table': jax.random.permutation(ks[6], n_pool)[:DEC_BATCH * n_pages]
                          .reshape(DEC_BATCH, n_pages).astype(jnp.int32),
        'w_in': nrm(ks[7], (DEPTH, D_MODEL, IN_COLS), D_MODEL ** -0.5) * jnp.asarray(col_scale),
        'cmp_pe': nrm(ks[8], (DEPTH, 2, BLOCK, HEAD_DIM), 0.1),
        'cmp_w1': nrm(ks[9], (DEPTH, 2, BLOCK, HEAD_DIM, HEAD_DIM), (BLOCK * HEAD_DIM) ** -0.5),
        'cmp_w2': nrm(ks[10], (DEPTH, 2, HEAD_DIM, HEAD_DIM), HEAD_DIM ** -0.5),
        'conv_w': nrm(ks[11], (DEPTH, CONV_WIDTH, D_RNN), CONV_WIDTH ** -0.5),
        'conv_b': nrm(ks[12], (DEPTH, D_RNN), 0.02),
        'lru_w_a': nrm(ks[13], (DEPTH, LRU_BLOCKS, LRU_BW, LRU_BW), LRU_BW ** -0.5),
        'lru_b_a': nrm(ks[14], (DEPTH, D_RNN), 0.02),
        'lru_w_i': nrm(ks[15], (DEPTH, LRU_BLOCKS, LRU_BW, LRU_BW), LRU_BW ** -0.5),
        'lru_b_i': nrm(ks[16], (DEPTH, D_RNN), 0.02),
        'lru_lam': jnp.log(a0) - jnp.log1p(-a0),
        'group_norm': 1.0 + nrm(ks[18], (DEPTH, D_MODEL), 0.02),
        'w_out': nrm(ks[19], (DEPTH, D_MODEL, D_MODEL), D_MODEL ** -0.5 * BETA),
        'ln_g': 1.0 + nrm(ks[20], (DEPTH, 2, D_MODEL), 0.02),
        'ln_b': nrm(ks[21], (DEPTH, 2, D_MODEL), 0.02),
        'router_w': nrm(ks[22], (D_MODEL, N_EXPERTS), D_MODEL ** -0.5),
        'router_b': nrm(ks[23], (N_EXPERTS,), 0.01),
        'exp_w_gate': nrm(ks[24], (DEPTH, N_EXPERTS, D_MODEL, D_EXPERT), D_MODEL ** -0.5),
        'exp_w_up': nrm(ks[25], (DEPTH, N_EXPERTS, D_MODEL, D_EXPERT), D_MODEL ** -0.5 * BETA),
        'exp_w_down': nrm(ks[26], (DEPTH, N_EXPERTS, D_EXPERT, D_MODEL), D_EXPERT ** -0.5 * BETA),
    }


def reference(x_prompt, x_sample, cache_kv, cache_win, state_conv, state_h, page_table,
              w_in, cmp_pe, cmp_w1, cmp_w2, conv_w, conv_b, lru_w_a, lru_b_a, lru_w_i, lru_b_i,
              lru_lam, group_norm, w_out, ln_g, ln_b, router_w, router_b,
              exp_w_gate, exp_w_up, exp_w_down):
    b, t, _ = x_prompt.shape
    db, ds, _ = x_sample.shape
    past_len = page_table.shape[1] * PAGE_SIZE
    xp, xs = x_prompt, x_sample
    kv_p, kv_s, win_p, win_s, conv_p, conv_s, h_p, h_s = [], [], [], [], [], [], [], []
    for l in range(DEPTH):
        lru_args = (conv_w[l], conv_b[l], lru_w_a[l], lru_b_a[l], lru_w_i[l], lru_b_i[l], lru_lam[l])
        moe_args = (router_w, router_b, exp_w_gate[l], exp_w_up[l], exp_w_down[l])

        q, kvs, gates, u, gg = in_proj(xp, w_in[l])
        attn = prompt_attention(q, gates, kvs, cmp_pe[l], cmp_w1[l], cmp_w2[l])
        rnn, cbuf, hl = rglru_branch(u, gg, jnp.zeros((b, CONV_WIDTH - 1, D_RNN), xp.dtype),
                                     jnp.zeros((b, D_RNN), jnp.float32), *lru_args)
        xp = layer_norm(ALPHA * xp + mixer_out(attn, rnn, group_norm[l], w_out[l]), ln_g[l, 0], ln_b[l, 0])
        xp = layer_norm(ALPHA * xp + grouped_moe(xp, *moe_args), ln_g[l, 1], ln_b[l, 1])
        kv_p.append(jnp.stack(kvs[:4], axis=2))
        win_p.append(jnp.stack(kvs[4:], axis=2)[:, -min(WINDOW, t):])
        conv_p.append(cbuf)
        h_p.append(hl)

        q, kvs, gates, u, gg = in_proj(xs, w_in[l])
        past = cache_kv[l][page_table].reshape(db, past_len, KV_SLOTS, KV_HEADS, HEAD_DIM)
        attn, wnew = sample_attention(q, gates, kvs, past, cache_win[l], cmp_pe[l], cmp_w1[l], cmp_w2[l])
        rnn, cbuf, hl = rglru_branch(u, gg, state_conv[l], state_h[l], *lru_args)
        xs = layer_norm(ALPHA * xs + mixer_out(attn, rnn, group_norm[l], w_out[l]), ln_g[l, 0], ln_b[l, 0])
        xs = layer_norm(ALPHA * xs + grouped_moe(xs, *moe_args), ln_g[l, 1], ln_b[l, 1])
        kv_s.append(jnp.stack(kvs[:4], axis=2))
        win_s.append(wnew)
        conv_s.append(cbuf)
        h_s.append(hl)

    return (xp, xs, jnp.stack(kv_p), jnp.stack(kv_s), jnp.stack(win_p), jnp.stack(win_s),
            jnp.stack(conv_p), jnp.stack(conv_s), jnp.stack(h_p), jnp.stack(h_s))
```

```python
import functools

import jax
import jax.numpy as jnp
from jax import lax
from jax.experimental import pallas as pl
from jax.experimental.pallas import tpu as pltpu

F32 = jnp.float32
BF16 = jnp.bfloat16
I32 = jnp.int32

D_MODEL = 2048
DEPTH = 4
PAGE_SIZE = 128
D_ATTN = D_MODEL // 2
D_RNN = D_MODEL - D_ATTN
N_HEADS = 8
HEAD_DIM = D_ATTN // N_HEADS
KV_HEADS = 2
Q_PER_KV = N_HEADS // KV_HEADS
BLOCK = 64
N_SELECT = 16
WINDOW = 512
FORCED_SCORE = 1e4
KV_SLOTS = 4
CONV_WIDTH = 4
LRU_BLOCKS = 8
LRU_BW = D_RNN // LRU_BLOCKS
LRU_C = 8.0
N_EXPERTS = 16
N_GROUPS = 4
EXPERTS_PER_GROUP = N_EXPERTS // N_GROUPS
D_EXPERT = 512
ALPHA = (2 * DEPTH) ** 0.25
LN_EPS = 1e-5
SCALE = HEAD_DIM ** -0.5
NEG = -1e30

LANE = 128
SUBLANE = 8
MOE_CHUNK = 128
VMEM_LIMIT = 56 * 1024 * 1024

_NT = (((1,), (1,)), ((), ()))
_TN = (((0,), (0,)), ((), ()))


def _params(n_axes):
    return pltpu.CompilerParams(dimension_semantics=("arbitrary",) * n_axes,
                                vmem_limit_bytes=VMEM_LIMIT)


def _sigmoid(x):
    return 1.0 / (1.0 + jnp.exp(-x))


def _gelu(x):
    return 0.5 * x * (1.0 + jnp.tanh(0.7978845608028654 * (x + 0.044715 * (x * x * x))))


def _dot(a, b):
    return jnp.dot(a, b, preferred_element_type=F32)


def _resident(shape):
    nd = len(shape)
    return pl.BlockSpec(shape, lambda *_: (0,) * nd, pipeline_mode=pl.Buffered(1))


def _in_proj_kernel(x_ref, w_ref, wg_ref, q_ref, kv4_ref, kwin_ref, kvb_ref, gates_ref, u_ref, gg_ref):
    xb = x_ref[...].astype(BF16)

    def mm(lo, hi):
        return _dot(xb, w_ref[:, lo:hi])

    q_ref[...] = mm(0, 1024).astype(BF16)
    kv = mm(1024, 2048)
    kv4_ref[...] = kv
    kw = mm(2048, 2560)
    kwin_ref[...] = kw
    kvb_ref[:, 0:512] = kv[:, 512:1024].astype(BF16)
    kvb_ref[:, 512:1024] = kw.astype(BF16)
    gates_ref[...] = _sigmoid(_dot(xb, wg_ref[...]))
    u_ref[...] = mm(2560, 3584)
    gg_ref[...] = mm(3584, 4608)


def _in_proj(x, w_main, w_gate, tm):
    n = x.shape[0]
    row = lambda c: pl.BlockSpec((tm, c), lambda i: (i, 0))
    return pl.pallas_call(
        _in_proj_kernel,
        grid=(n // tm,),
        in_specs=[row(D_MODEL), _resident(w_main.shape), _resident(w_gate.shape)],
        out_specs=[row(1024), row(1024), row(512), row(1024), row(256), row(1024), row(1024)],
        out_shape=[jax.ShapeDtypeStruct((n, 1024), BF16),
                   jax.ShapeDtypeStruct((n, 1024), F32),
                   jax.ShapeDtypeStruct((n, 512), F32),
                   jax.ShapeDtypeStruct((n, 1024), BF16),
                   jax.ShapeDtypeStruct((n, 256), F32),
                   jax.ShapeDtypeStruct((n, 1024), F32),
                   jax.ShapeDtypeStruct((n, 1024), F32)],
        compiler_params=_params(1),
        name="in_proj",
    )(x, w_main, w_gate)


def _compress_kernel(x0_ref, x1_ref, x2_ref, x3_ref, pe_ref, w1_ref, w2_ref, o_ref, *, nblk):
    x_refs = (x0_ref, x1_ref, x2_ref, x3_ref)
    for s in range(2):
        acc = [jnp.zeros((nblk, HEAD_DIM), F32) for _ in range(KV_HEADS)]
        for l in range(BLOCK):
            pe = pe_ref[s, l:l + 1, :]
            w = w1_ref[s, l]
            for g in range(KV_HEADS):
                rows = x_refs[s * KV_HEADS + g][pl.ds(l, nblk, stride=BLOCK), :]
                acc[g] = acc[g] + _dot((rows + pe).astype(BF16), w)
        for g in range(KV_HEADS):
            h = _gelu(acc[g]).astype(BF16)
            o_ref[s * KV_HEADS + g] = _dot(h, w2_ref[s]).astype(BF16)


def _compress(kv4, pe, w1, w2, batch, t_len):
    nblk = t_len // BLOCK
    col = lambda c: pl.BlockSpec((t_len, HEAD_DIM), lambda b: (b, c))
    return pl.pallas_call(
        functools.partial(_compress_kernel, nblk=nblk),
        grid=(batch,),
        in_specs=[col(0), col(1), col(2), col(3),
                  _resident(pe.shape), _resident(w1.shape), _resident(w2.shape)],
        out_specs=pl.BlockSpec((None, 4, nblk, HEAD_DIM), lambda b: (b, 0, 0, 0)),
        out_shape=jax.ShapeDtypeStruct((batch, 4, nblk, HEAD_DIM), BF16),
        compiler_params=_params(1),
        name="compress",
    )(kv4, kv4, kv4, kv4, pe, w1, w2)


def _attn_kernel(q_ref, gates_ref, kc_ref, vc_ref, ks_ref, vs_ref, kw_ref, vw_ref, e_ref, o_ref,
                 *, tq, tk, t_len):
    i = pl.program_id(2)
    s0 = i * tq
    nblk = t_len // BLOCK
    qpos = s0 + lax.broadcasted_iota(I32, (tq, 1), 0)
    qh = [q_ref[:, r * HEAD_DIM:(r + 1) * HEAD_DIM] for r in range(Q_PER_KV)]

    blk = lax.broadcasted_iota(I32, (nblk, tq), 0)
    qpos_l = s0 + lax.broadcasted_iota(I32, (1, tq), 1)
    cmask = (blk + 1) * BLOCK - 1 <= qpos_l
    kc = kc_ref[...]
    vc = vc_ref[...]
    imp = jnp.zeros((nblk, tq), F32)
    o_cmp = []
    for r in range(Q_PER_KV):
        sc = jnp.where(cmask, lax.dot_general(kc, qh[r], _NT, preferred_element_type=F32) * SCALE, NEG)
        m = jnp.max(sc, axis=0, keepdims=True)
        e = jnp.where(cmask, jnp.exp(sc - m), 0.0)
        p = e / jnp.maximum(jnp.sum(e, axis=0, keepdims=True), 1e-30)
        imp = imp + p
        o_cmp.append(lax.dot_general(p.astype(BF16), vc, _TN, preferred_element_type=F32))

    cur = jnp.right_shift(qpos_l, 6)
    forced = (blk == 0) | (blk == cur) | (blk == cur - 1)
    valid = blk <= cur
    score = jnp.where(valid, jnp.where(forced, FORCED_SCORE, imp), -1.0)
    rank = jnp.zeros((nblk, tq), I32)
    for j in range(nblk):
        cj = score[j:j + 1, :]
        beats = (cj > score) | ((cj == score) & (blk > j))
        rank = rank + beats.astype(I32)
    sel = ((rank < min(N_SELECT, nblk)) & valid).astype(BF16)

    def sel_body(kt, carry):
        off = pl.multiple_of(kt * tk, tk)
        k = ks_ref[pl.ds(off, tk), :]
        v = vs_ref[pl.ds(off, tk), :]
        hit = lax.dot_general(sel, e_ref[:, pl.ds(off, tk)], _TN, preferred_element_type=F32)
        kpos = off + lax.broadcasted_iota(I32, (tq, tk), 1)
        bias = jnp.where((hit > 0.5) & (kpos <= qpos), 0.0, NEG)
        out = []
        for r in range(Q_PER_KV):
            m, l, acc = carry[r]
            s = lax.dot_general(qh[r], k, _NT, preferred_element_type=F32) * SCALE + bias
            m_new = jnp.maximum(m, jnp.max(s, axis=1, keepdims=True))
            a = jnp.exp(m - m_new)
            p = jnp.exp(s - m_new)
            l = a * l + jnp.sum(p, axis=1, keepdims=True)
            acc = a * acc + _dot(p.astype(BF16), v)
            out.append((m_new, l, acc))
        return tuple(out)

    init = tuple((jnp.full((tq, 1), NEG, F32), jnp.zeros((tq, 1), F32), jnp.zeros((tq, HEAD_DIM), F32))
                 for _ in range(Q_PER_KV))
    n_t = (s0 + tq + tk - 1) // tk
    fin = lax.fori_loop(0, n_t, sel_body, init)
    o_sel = [acc / l for (_, l, acc) in fin]

    wlen = WINDOW + tq
    start = pl.multiple_of(jnp.maximum(s0 - WINDOW, 0), tq)
    kw = kw_ref[pl.ds(start, wlen), :]
    vw = vw_ref[pl.ds(start, wlen), :]
    kposw = start + lax.broadcasted_iota(I32, (tq, wlen), 1)
    biasw = jnp.where((kposw <= qpos) & (kposw > qpos - WINDOW), 0.0, NEG)

    for r in range(Q_PER_KV):
        s = lax.dot_general(qh[r], kw, _NT, preferred_element_type=F32) * SCALE + biasw
        p = jnp.exp(s - jnp.max(s, axis=1, keepdims=True))
        o_win = _dot(p.astype(BF16), vw) / jnp.sum(p, axis=1, keepdims=True)
        g_cmp = gates_ref[:, r:r + 1]
        g_sel = gates_ref[:, Q_PER_KV + r:Q_PER_KV + r + 1]
        g_win = gates_ref[:, 2 * Q_PER_KV + r:2 * Q_PER_KV + r + 1]
        o_ref[:, r * HEAD_DIM:(r + 1) * HEAD_DIM] = g_cmp * o_cmp[r] + g_sel * o_sel[r] + g_win * o_win


def _prompt_attention(q, gates, kvc, kvb, expand, batch, t_len, tq, tk):
    nq = t_len // tq
    gw = Q_PER_KV * HEAD_DIM
    col = lambda c: pl.BlockSpec((t_len, HEAD_DIM), lambda b, g, i: (b, c + g))
    cmp_blk = lambda c: pl.BlockSpec((None, None, t_len // BLOCK, HEAD_DIM), lambda b, g, i: (b, c + g, 0, 0))
    return pl.pallas_call(
        functools.partial(_attn_kernel, tq=tq, tk=tk, t_len=t_len),
        grid=(batch, KV_HEADS, nq),
        in_specs=[pl.BlockSpec((tq, gw), lambda b, g, i: (b * nq + i, g)),
                  pl.BlockSpec((tq, LANE), lambda b, g, i: (b * nq + i, g)),
                  cmp_blk(0), cmp_blk(2),
                  col(0), col(2), col(4), col(6),
                  pl.BlockSpec(expand.shape, lambda b, g, i: (0, 0))],
        out_specs=pl.BlockSpec((tq, gw), lambda b, g, i: (b * nq + i, g)),
        out_shape=jax.ShapeDtypeStruct((batch * t_len, D_ATTN), F32),
        compiler_params=_params(3),
        name="prompt_attention",
    )(q, gates, kvc, kvc, kvb, kvb, kvb, kvb, expand)


def _rglru_kernel(u_ref, gg_ref, c0_ref, h0_ref, cw_ref, cb_ref, wai_ref, ba_ref, bi_ref, lam_ref,
                  y_ref, hl_ref, ext_ref, a_ref, b_ref, tail_ref, h_ref, *, tt):
    t = pl.program_id(1)

    @pl.when(t == 0)
    def _():
        tail_ref[...] = c0_ref[...]
        h_ref[...] = h0_ref[...]

    u = u_ref[...]
    ext_ref[0:SUBLANE, :] = tail_ref[...]
    ext_ref[SUBLANE:SUBLANE + tt, :] = u
    base = SUBLANE - (CONV_WIDTH - 1)
    xc = cb_ref[...] + ext_ref[base:base + tt, :] * cw_ref[0:1, :]
    for j in range(1, CONV_WIDTH - 1):
        xc = xc + ext_ref[base + j:base + j + tt, :] * cw_ref[j:j + 1, :]
    xc = xc + u * cw_ref[CONV_WIDTH - 1:CONV_WIDTH, :]
    tail_ref[...] = ext_ref[tt:tt + SUBLANE, :]

    xcb = xc.astype(BF16)
    for n in range(LRU_BLOCKS):
        c = slice(n * LRU_BW, (n + 1) * LRU_BW)
        ai = _dot(xcb[:, c], wai_ref[n])
        r = _sigmoid(ai[:, :LRU_BW] + ba_ref[:, c])
        ig = _sigmoid(ai[:, LRU_BW:] + bi_ref[:, c])
        nl = -lam_ref[:, c]
        softplus = jnp.maximum(nl, 0.0) + jnp.log1p(jnp.exp(-jnp.abs(nl)))
        log_a = (-LRU_C * softplus) * r
        a = jnp.exp(log_a)
        gain = jnp.sqrt(-jnp.tanh(log_a) * (a * a + 1.0))
        a_ref[:, c] = a
        b_ref[:, c] = gain * (ig * xc[:, c])

    def group(gi, h):
        off = pl.multiple_of(gi * SUBLANE, SUBLANE)
        a8 = a_ref[pl.ds(off, SUBLANE), :]
        b8 = b_ref[pl.ds(off, SUBLANE), :]
        rows = []
        for r in range(SUBLANE):
            h = a8[r:r + 1, :] * h + b8[r:r + 1, :]
            rows.append(h)
        a_ref[pl.ds(off, SUBLANE), :] = jnp.concatenate(rows, axis=0)
        return h

    h = lax.fori_loop(0, tt // SUBLANE, group, h_ref[0:1, :])
    h_ref[0:1, :] = h
    hl_ref[...] = h
    y_ref[...] = a_ref[...] * _gelu(gg_ref[...])


def _rglru(u, gg, conv0, h0, conv_w, conv_b, w_ai, b_a, b_i, lam, batch, t_len, tt):
    nt = t_len // tt
    c = D_RNN
    tile = pl.BlockSpec((tt, c), lambda b, t: (b * nt + t, 0))
    vec = _resident((1, c))
    return pl.pallas_call(
        functools.partial(_rglru_kernel, tt=tt),
        grid=(batch, nt),
        in_specs=[tile, tile,
                  pl.BlockSpec((None, SUBLANE, c), lambda b, t: (b, 0, 0)),
                  pl.BlockSpec((None, SUBLANE, c), lambda b, t: (b, 0, 0)),
                  _resident(conv_w.shape), vec, _resident(w_ai.shape), vec, vec, vec],
        out_specs=[tile, pl.BlockSpec((None, 1, c), lambda b, t: (b, 0, 0))],
        out_shape=[jax.ShapeDtypeStruct((batch * t_len, c), F32),
                   jax.ShapeDtypeStruct((batch, 1, c), F32)],
        scratch_shapes=[pltpu.VMEM((tt + SUBLANE, c), F32), pltpu.VMEM((tt, c), F32),
                        pltpu.VMEM((tt, c), F32), pltpu.VMEM((SUBLANE, c), F32),
                        pltpu.VMEM((SUBLANE, c), F32)],
        compiler_params=_params(2),
        name="rglru",
    )(u, gg, conv0, h0, conv_w, conv_b, w_ai, b_a, b_i, lam)


def _layer_norm(z, g, b):
    zc = z - jnp.mean(z, axis=1, keepdims=True)
    var = jnp.mean(zc * zc, axis=1, keepdims=True)
    return zc * lax.rsqrt(var + LN_EPS) * g + b


def _mixer_kernel(attn_ref, rnn_ref, x_ref, gn_ref, wo_ref, lng_ref, lnb_ref, rw_ref, rb_ref,
                  x1_ref, x1b_ref, rt_ref):
    def rms(v, g):
        return (v * lax.rsqrt(jnp.mean(v * v, axis=1, keepdims=True) + LN_EPS) * g).astype(BF16)

    an = rms(attn_ref[...], gn_ref[:, 0:D_ATTN])
    rn = rms(rnn_ref[...], gn_ref[:, D_ATTN:D_MODEL])
    y = _dot(an, wo_ref[0:D_ATTN, :]) + _dot(rn, wo_ref[D_ATTN:D_MODEL, :])
    x1 = _layer_norm(ALPHA * x_ref[...] + y, lng_ref[...], lnb_ref[...])
    x1_ref[...] = x1
    x1b = x1.astype(BF16)
    x1b_ref[...] = x1b

    s = _sigmoid(_dot(x1b, rw_ref[...]))
    biased = s + rb_ref[...]
    sc = [s[:, e:e + 1] for e in range(N_EXPERTS)]
    bc = [biased[:, e:e + 1] for e in range(N_EXPERTS)]
    best = grp = None
    for g in range(N_GROUPS):
        v = bc[g * EXPERTS_PER_GROUP:(g + 1) * EXPERTS_PER_GROUP]
        top2 = None
        for i in range(EXPERTS_PER_GROUP):
            for j in range(i + 1, EXPERTS_PER_GROUP):
                pair = v[i] + v[j]
                top2 = pair if top2 is None else jnp.maximum(top2, pair)
        if g == 0:
            best, grp = top2, jnp.zeros_like(top2, dtype=I32)
        else:
            better = top2 > best
            grp = jnp.where(better, g, grp)
            best = jnp.where(better, top2, best)

    def in_group(cols, j):
        out = cols[j]
        for g in range(1, N_GROUPS):
            out = jnp.where(grp == g, cols[g * EXPERTS_PER_GROUP + j], out)
        return out

    bv = [in_group(bc, j) for j in range(EXPERTS_PER_GROUP)]
    sv = [in_group(sc, j) for j in range(EXPERTS_PER_GROUP)]

    def argmax_first(vals, skip=None):
        bestv = idx = None
        for j, v in enumerate(vals):
            vj = v if skip is None else jnp.where(skip == j, -jnp.inf, v)
            if j == 0:
                bestv, idx = vj, jnp.zeros_like(grp)
            else:
                better = vj > bestv
                idx = jnp.where(better, j, idx)
                bestv = jnp.where(better, vj, bestv)
        return idx

    i1 = argmax_first(bv)
    i2 = argmax_first(bv, skip=i1)

    def pick(vals, idx):
        out = vals[0]
        for j in range(1, len(vals)):
            out = jnp.where(idx == j, vals[j], out)
        return out

    s1 = pick(sv, i1)
    s2 = pick(sv, i2)
    den = s1 + s2
    w1 = s1 / den
    w2 = s2 / den
    lane = lax.broadcasted_iota(I32, rt_ref.shape, 1)
    rt = jnp.where(lane == EXPERTS_PER_GROUP, grp.astype(F32), 0.0)
    rt = rt + jnp.where(lane == i1, w1, 0.0) + jnp.where(lane == i2, w2, 0.0)
    rt_ref[...] = rt


def _mixer(attn, rnn, x, gn, w_out, ln_g, ln_b, rw, rb, tm):
    n = x.shape[0]
    row = lambda c: pl.BlockSpec((tm, c), lambda i: (i, 0))
    vec = _resident((1, D_MODEL))
    return pl.pallas_call(
        _mixer_kernel,
        grid=(n // tm,),
        in_specs=[row(D_ATTN), row(D_RNN), row(D_MODEL), vec, _resident(w_out.shape), vec, vec,
                  _resident(rw.shape), _resident(rb.shape)],
        out_specs=[row(D_MODEL), row(D_MODEL), row(LANE)],
        out_shape=[jax.ShapeDtypeStruct((n, D_MODEL), F32),
                   jax.ShapeDtypeStruct((n, D_MODEL), BF16),
                   jax.ShapeDtypeStruct((n, LANE), F32)],
        compiler_params=_params(1),
        name="mixer_out",
    )(attn, rnn, x, gn, w_out, ln_g, ln_b, rw, rb)


def _split3(v):
    a = v.astype(BF16)
    r = v - a.astype(F32)
    b = r.astype(BF16)
    c = (r - b.astype(F32)).astype(BF16)
    return a, b, c


def _moe_kernel(*refs, tm, group_axis):
    if group_axis == 0:
        x_ref, rt_ref, acc_ref, ltri_ref, wg_ref, wu_ref, wd_ref, o_ref = refs
    else:
        x_ref, rt_ref, ltri_ref, wg_ref, wu_ref, wd_ref, o_ref = refs
    g = pl.program_id(group_axis)
    rt = rt_ref[...]
    member = rt[:, EXPERTS_PER_GROUP:EXPERTS_PER_GROUP + 1] == g.astype(F32)
    mf = member.astype(F32)
    pos = _dot(ltri_ref[...], jnp.broadcast_to(mf, (tm, LANE)).astype(BF16))[:, 0:1].astype(I32)
    count = jnp.sum(mf).astype(I32)
    xb = x_ref[...]
    c1, c2, c3 = _split3(rt)
    lane = lax.broadcasted_iota(I32, (tm, MOE_CHUNK), 1)
    if group_axis == 0:
        o_ref[...] = acc_ref[...]
    else:
        @pl.when(g == 0)
        def _():
            o_ref[...] = jnp.zeros_like(o_ref)

    def chunk(k, carry):
        ptf = jnp.where(member & (pos - k * MOE_CHUNK == lane), 1.0, 0.0)
        pt = ptf.astype(BF16)
        p = ptf.T.astype(BF16)
        xc = _dot(p, xb).astype(BF16)
        cw = _dot(p, c1) + _dot(p, c2) + _dot(p, c3)
        y = jnp.zeros((MOE_CHUNK, D_MODEL), F32)
        for j in range(EXPERTS_PER_GROUP):
            hg = _dot(xc, wg_ref[j])
            hu = _dot(xc, wu_ref[j])
            h = (hg * _sigmoid(hg)) * hu * cw[:, j:j + 1]
            y = y + _dot(h.astype(BF16), wd_ref[j])
        yh = y.astype(BF16)
        yl = (y - yh.astype(F32)).astype(BF16)
        o_ref[...] += _dot(pt, yh) + _dot(pt, yl)
        return carry

    lax.fori_loop(0, (count + MOE_CHUNK - 1) // MOE_CHUNK, chunk, 0)


def _moe_single_tile(x, rt, ltri, wg, wu, wd):
    n = x.shape[0]
    row = lambda c: pl.BlockSpec((n, c), lambda t, g: (0, 0))
    grp = lambda a, b: pl.BlockSpec((EXPERTS_PER_GROUP, a, b), lambda t, g: (g, 0, 0))
    return pl.pallas_call(
        functools.partial(_moe_kernel, tm=n, group_axis=1),
        grid=(1, N_GROUPS),
        in_specs=[row(D_MODEL), row(LANE), pl.BlockSpec(ltri.shape, lambda t, g: (0, 0)),
                  grp(D_MODEL, D_EXPERT), grp(D_MODEL, D_EXPERT), grp(D_EXPERT, D_MODEL)],
        out_specs=row(D_MODEL),
        out_shape=jax.ShapeDtypeStruct((n, D_MODEL), F32),
        compiler_params=_params(2),
        name="grouped_moe_single_tile",
    )(x, rt, ltri, wg, wu, wd)


def _moe(x, rt, ltri, wg, wu, wd, tm):
    n = x.shape[0]
    assert n // tm >= 4
    row = lambda c: pl.BlockSpec((tm, c), lambda g, t: (t, 0))
    grp = lambda a, b: pl.BlockSpec((EXPERTS_PER_GROUP, a, b), lambda g, t: (g, 0, 0),
                                    pipeline_mode=pl.Buffered(1))
    return pl.pallas_call(
        functools.partial(_moe_kernel, tm=tm, group_axis=0),
        grid=(N_GROUPS, n // tm),
        in_specs=[row(D_MODEL), row(LANE), row(D_MODEL), _resident(ltri.shape),
                  grp(D_MODEL, D_EXPERT), grp(D_MODEL, D_EXPERT), grp(D_EXPERT, D_MODEL)],
        out_specs=row(D_MODEL),
        out_shape=jax.ShapeDtypeStruct((n, D_MODEL), F32),
        input_output_aliases={2: 0},
        compiler_params=_params(2),
        name="grouped_moe",
    )(x, rt, jnp.zeros((n, D_MODEL), F32), ltri, wg, wu, wd)


def _ln2_kernel(x_ref, y_ref, g_ref, b_ref, o_ref):
    o_ref[...] = _layer_norm(ALPHA * x_ref[...] + y_ref[...], g_ref[...], b_ref[...])


def _ln2(x, y, g, b, tm):
    n = x.shape[0]
    row = pl.BlockSpec((tm, D_MODEL), lambda i: (i, 0))
    vec = _resident((1, D_MODEL))
    return pl.pallas_call(
        _ln2_kernel, grid=(n // tm,), in_specs=[row, row, vec, vec], out_specs=row,
        out_shape=jax.ShapeDtypeStruct((n, D_MODEL), F32),
        compiler_params=_params(1), name="moe_residual_ln",
    )(x, y, g, b)


PAGES_PER_STEP = 32
ROWS_PER_STEP = PAGES_PER_STEP * PAGE_SIZE
BLOCKS_PER_STEP = ROWS_PER_STEP // BLOCK
HALF = KV_HEADS * HEAD_DIM * 2


def _page_specs(layer, half, n_pages):
    def spec(j):
        return pl.BlockSpec((None, None, PAGE_SIZE, HALF),
                            lambda b, s, pt: (layer, pt[b * n_pages + s * PAGES_PER_STEP + j], 0, half))
    return [spec(j) for j in range(PAGES_PER_STEP)]


def _sample_compress_kernel(pt_ref, *refs):
    pages = refs[:PAGES_PER_STEP]
    pe_ref, w1_ref, w2_ref, o_ref, rows_ref = refs[PAGES_PER_STEP:]
    for j, page in enumerate(pages):
        for c in range(2 * KV_HEADS):
            rows_ref[c, j * PAGE_SIZE:(j + 1) * PAGE_SIZE, :] = page[:, c * HEAD_DIM:(c + 1) * HEAD_DIM]
    for s in range(2):
        acc = jnp.zeros((KV_HEADS * BLOCKS_PER_STEP, HEAD_DIM), F32)
        for lp in range(BLOCK // 2):
            parts = []
            for l in (2 * lp, 2 * lp + 1):
                pe = pe_ref[s, l:l + 1, :]
                parts.append(jnp.concatenate(
                    [rows_ref[s * KV_HEADS + g, pl.ds(l, BLOCKS_PER_STEP, stride=BLOCK), :] + pe
                     for g in range(KV_HEADS)], axis=0))
            lhs = jnp.concatenate(parts, axis=1).astype(BF16)
            acc = acc + _dot(lhs, w1_ref[s, lp])
        out = _dot(_gelu(acc).astype(BF16), w2_ref[s]).astype(BF16)
        for g in range(KV_HEADS):
            o_ref[s * KV_HEADS + g] = out[g * BLOCKS_PER_STEP:(g + 1) * BLOCKS_PER_STEP]


def _sample_compress(page_table, cache, layer, pe, w1_pairs, w2, db, n_pages):
    n_steps = n_pages // PAGES_PER_STEP
    nblk = n_pages * PAGE_SIZE // BLOCK
    const = lambda shape: pl.BlockSpec(shape, lambda b, s, pt: (0,) * len(shape))
    return pl.pallas_call(
        _sample_compress_kernel,
        grid_spec=pltpu.PrefetchScalarGridSpec(
            num_scalar_prefetch=1, grid=(db, n_steps),
            in_specs=_page_specs(layer, 0, n_pages) + [const(pe.shape), const(w1_pairs.shape), const(w2.shape)],
            out_specs=pl.BlockSpec((None, 4, BLOCKS_PER_STEP, HEAD_DIM), lambda b, s, pt: (b, 0, s, 0)),
            scratch_shapes=[pltpu.VMEM((2 * KV_HEADS, ROWS_PER_STEP, HEAD_DIM), F32)]),
        out_shape=jax.ShapeDtypeStruct((db, 4, nblk, HEAD_DIM), BF16),
        compiler_params=_params(2),
        name="sample_compress",
    )(page_table, *([cache] * PAGES_PER_STEP), pe, w1_pairs, w2)


def _softmax_update(s, v, m, l, acc):
    m_new = jnp.maximum(m, jnp.max(s, axis=1, keepdims=True))
    a = jnp.exp(m - m_new)
    p = jnp.exp(s - m_new)
    return m_new, a * l + jnp.sum(p, axis=1, keepdims=True), a * acc + _dot(p.astype(BF16), v)


def _sample_attn_kernel(pt_ref, *refs, ds, past_len, n_steps):
    pages = refs[:PAGES_PER_STEP]
    (q_ref, gates_ref, kvc_ref, new_ref, nwin_ref, cwin_ref, e_ref,
     o_ref, wout_ref, k_ref, v_ref, sel_ref, ocmp_ref, m_ref, l_ref, acc_ref) = refs[PAGES_PER_STEP:]
    step = pl.program_id(1)
    rows = Q_PER_KV * ds
    nc = past_len // BLOCK
    ns = -(-(past_len + ds) // BLOCK)
    ns_pad = -(-ns // LANE) * LANE
    qi = lax.broadcasted_iota(I32, (rows, 1), 0) % ds
    qpos = past_len + qi

    def q_group(g):
        return jnp.concatenate([q_ref[:, (g * Q_PER_KV + r) * HEAD_DIM:(g * Q_PER_KV + r + 1) * HEAD_DIM]
                                for r in range(Q_PER_KV)], axis=0).astype(BF16)

    @pl.when(step == 0)
    def _():
        blk_c = lax.broadcasted_iota(I32, (rows, nc), 1)
        cmask = (blk_c + 1) * BLOCK - 1 <= qpos
        blk = lax.broadcasted_iota(I32, (ds, ns_pad), 1)
        qpos_q = past_len + lax.broadcasted_iota(I32, (ds, 1), 0)
        cur = qpos_q // BLOCK
        forced = (blk == 0) | (blk == cur) | (blk == cur - 1)
        valid = (blk <= cur) & (blk < ns)
        for g in range(KV_HEADS):
            qg = q_group(g)
            sc = jnp.where(cmask, lax.dot_general(qg, kvc_ref[g], _NT, preferred_element_type=F32) * SCALE, NEG)
            m = jnp.max(sc, axis=1, keepdims=True)
            e = jnp.where(cmask, jnp.exp(sc - m), 0.0)
            p = e / jnp.maximum(jnp.sum(e, axis=1, keepdims=True), 1e-30)
            ocmp_ref[g] = _dot(p.astype(BF16), kvc_ref[KV_HEADS + g])
            imp = p[0:ds]
            for r in range(1, Q_PER_KV):
                imp = imp + p[r * ds:(r + 1) * ds]
            imp = jnp.concatenate([imp, jnp.zeros((ds, ns_pad - nc), F32)], axis=1)
            score = jnp.where(valid, jnp.where(forced, FORCED_SCORE, imp), -1.0)
            rank = jnp.zeros((ds, ns_pad), I32)
            for j in range(ns):
                cj = score[:, j:j + 1]
                rank = rank + ((cj > score) | ((cj == score) & (blk > j))).astype(I32)
            sel = ((rank < min(N_SELECT, ns)) & valid).astype(F32)
            sel = jnp.concatenate([sel] * Q_PER_KV, axis=0)
            for t in range(n_steps):
                part = sel[:, t * BLOCKS_PER_STEP:(t + 1) * BLOCKS_PER_STEP]
                sel_ref[g, t] = jnp.concatenate(
                    [part, jnp.zeros((rows, LANE - BLOCKS_PER_STEP), F32)], axis=1).astype(BF16)
            sel_ref[g, n_steps] = jnp.broadcast_to(sel[:, nc:nc + 1], (rows, LANE)).astype(BF16)
            m_ref[g] = jnp.full((rows, 1), NEG, F32)
            l_ref[g] = jnp.zeros((rows, 1), F32)
            acc_ref[g] = jnp.zeros((rows, HEAD_DIM), F32)

    for j, page in enumerate(pages):
        for g in range(KV_HEADS):
            k_ref[g, j * PAGE_SIZE:(j + 1) * PAGE_SIZE, :] = page[:, g * HEAD_DIM:(g + 1) * HEAD_DIM].astype(BF16)
            v_ref[g, j * PAGE_SIZE:(j + 1) * PAGE_SIZE, :] = page[:, (KV_HEADS + g) * HEAD_DIM:(KV_HEADS + g + 1) * HEAD_DIM].astype(BF16)
    for g in range(KV_HEADS):
        qg = q_group(g)
        hit = _dot(sel_ref[g, step], e_ref[...])
        s = lax.dot_general(qg, k_ref[g], _NT, preferred_element_type=F32) * SCALE
        s = jnp.where(hit > 0.5, s, NEG)
        m, l, acc = _softmax_update(s, v_ref[g], m_ref[g], l_ref[g], acc_ref[g])
        m_ref[g] = m
        l_ref[g] = l
        acc_ref[g] = acc

    @pl.when(step == n_steps - 1)
    def _():
        pad = jnp.zeros((2 * SUBLANE - ds, HEAD_DIM), F32)
        col = lax.broadcasted_iota(I32, (rows, 2 * SUBLANE), 1)
        new_ok = (col <= qi) & (col < ds)
        wpos = lax.broadcasted_iota(I32, (rows, WINDOW), 1)
        for g in range(KV_HEADS):
            qg = q_group(g)
            kn = jnp.concatenate([new_ref[:, (2 * KV_HEADS + g) * HEAD_DIM:(2 * KV_HEADS + g + 1) * HEAD_DIM], pad], axis=0).astype(BF16)
            vn = jnp.concatenate([new_ref[:, (3 * KV_HEADS + g) * HEAD_DIM:(3 * KV_HEADS + g + 1) * HEAD_DIM], pad], axis=0).astype(BF16)
            s = lax.dot_general(qg, kn, _NT, preferred_element_type=F32) * SCALE
            picked = sel_ref[g, n_steps][:, 0:2 * SUBLANE].astype(F32) > 0.5
            s = jnp.where(new_ok & picked, s, NEG)
            m, l, acc = _softmax_update(s, vn, m_ref[g], l_ref[g], acc_ref[g])
            o_sel = acc / l
            kw = cwin_ref[:, g * HEAD_DIM:(g + 1) * HEAD_DIM].astype(BF16)
            vw = cwin_ref[:, (KV_HEADS + g) * HEAD_DIM:(KV_HEADS + g + 1) * HEAD_DIM].astype(BF16)
            s = lax.dot_general(qg, kw, _NT, preferred_element_type=F32) * SCALE
            s = jnp.where(wpos > qi, s, NEG)
            m, l, acc = _softmax_update(s, vw, jnp.full((rows, 1), NEG, F32), jnp.zeros((rows, 1), F32),
                                        jnp.zeros((rows, HEAD_DIM), F32))
            kn = jnp.concatenate([nwin_ref[:, g * HEAD_DIM:(g + 1) * HEAD_DIM], pad], axis=0).astype(BF16)
            vn = jnp.concatenate([nwin_ref[:, (KV_HEADS + g) * HEAD_DIM:(KV_HEADS + g + 1) * HEAD_DIM], pad], axis=0).astype(BF16)
            s = lax.dot_general(qg, kn, _NT, preferred_element_type=F32) * SCALE
            s = jnp.where(new_ok, s, NEG)
            m, l, acc = _softmax_update(s, vn, m, l, acc)
            o_win = acc / l
            o_cmp = ocmp_ref[g]
            for r in range(Q_PER_KV):
                rs = slice(r * ds, (r + 1) * ds)
                c0 = g * LANE
                g_cmp = gates_ref[:, c0 + r:c0 + r + 1]
                g_sel = gates_ref[:, c0 + Q_PER_KV + r:c0 + Q_PER_KV + r + 1]
                g_win = gates_ref[:, c0 + 2 * Q_PER_KV + r:c0 + 2 * Q_PER_KV + r + 1]
                h = g * Q_PER_KV + r
                o_ref[:, h * HEAD_DIM:(h + 1) * HEAD_DIM] = g_cmp * o_cmp[rs] + g_sel * o_sel[rs] + g_win * o_win[rs]
        wout_ref[0:WINDOW - ds, :] = cwin_ref[ds:WINDOW, :]
        wout_ref[WINDOW - ds:WINDOW, :] = nwin_ref[...]


def _sample_attention(page_table, cache, layer, q, gates, kvc, kv4, kwin, cache_win, expand, db, ds, n_pages):
    n_steps = n_pages // PAGES_PER_STEP
    past_len = n_pages * PAGE_SIZE
    assert ds == SUBLANE and past_len % BLOCK == 0 and past_len >= WINDOW and cache_win.shape[2] == WINDOW
    rows = Q_PER_KV * ds
    per_b = lambda c: pl.BlockSpec((ds, c), lambda b, s, pt: (b, 0))
    return pl.pallas_call(
        functools.partial(_sample_attn_kernel, ds=ds, past_len=past_len, n_steps=n_steps),
        grid_spec=pltpu.PrefetchScalarGridSpec(
            num_scalar_prefetch=1, grid=(db, n_steps),
            in_specs=_page_specs(layer, 1, n_pages) + [
                per_b(D_ATTN), per_b(KV_HEADS * LANE),
                pl.BlockSpec((None, 4, past_len // BLOCK, HEAD_DIM), lambda b, s, pt: (b, 0, 0, 0)),
                per_b(2 * HALF), per_b(HALF),
                pl.BlockSpec((None, None, WINDOW, HALF), lambda b, s, pt: (layer, b, 0, 0)),
                pl.BlockSpec(expand.shape, lambda b, s, pt: (0, 0))],
            out_specs=[per_b(D_ATTN), pl.BlockSpec((None, WINDOW, HALF), lambda b, s, pt: (b, 0, 0))],
            scratch_shapes=[pltpu.VMEM((KV_HEADS, ROWS_PER_STEP, HEAD_DIM), BF16),
                            pltpu.VMEM((KV_HEADS, ROWS_PER_STEP, HEAD_DIM), BF16),
                            pltpu.VMEM((KV_HEADS, n_steps + 1, rows, LANE), BF16),
                            pltpu.VMEM((KV_HEADS, rows, HEAD_DIM), F32),
                            pltpu.VMEM((KV_HEADS, rows, 1), F32),
                            pltpu.VMEM((KV_HEADS, rows, 1), F32),
                            pltpu.VMEM((KV_HEADS, rows, HEAD_DIM), F32)]),
        out_shape=[jax.ShapeDtypeStruct((db * ds, D_ATTN), F32),
                   jax.ShapeDtypeStruct((db, WINDOW, HALF), F32)],
        compiler_params=_params(2),
        name="sample_attention",
    )(page_table, *([cache] * PAGES_PER_STEP), q, gates, kvc, kv4, kwin, cache_win, expand)


def _gate_columns(w_gates):
    d = w_gates.shape[0]
    w = w_gates.reshape(d, 3, KV_HEADS, Q_PER_KV).transpose(0, 2, 1, 3).reshape(d, KV_HEADS, 3 * Q_PER_KV)
    return jnp.pad(w, ((0, 0), (0, 0), (0, LANE - 3 * Q_PER_KV))).reshape(d, KV_HEADS * LANE)


def kernel(x_prompt, x_sample, cache_kv, cache_win, state_conv, state_h, page_table, w_in, cmp_pe, cmp_w1, cmp_w2, conv_w, conv_b, lru_w_a, lru_b_a, lru_w_i, lru_b_i, lru_lam, group_norm, w_out, ln_g, ln_b, router_w, router_b, exp_w_gate, exp_w_up, exp_w_down):
    b, t, _ = x_prompt.shape
    db, ds, _ = x_sample.shape
    past_len = page_table.shape[1] * PAGE_SIZE
    n_p, n_s = b * t, db * ds
    tm_p, tm_moe, tq, tk, tt = 256, 512, 256, 1024, 256
    n_s_pad = -(-n_s // MOE_CHUNK) * MOE_CHUNK

    xp = x_prompt.reshape(n_p, D_MODEL)
    xs = x_sample.reshape(n_s, D_MODEL)
    expand = (jnp.arange(t)[None, :] // BLOCK == jnp.arange(t // BLOCK)[:, None]).astype(BF16)
    ltri = lambda m: (jnp.arange(m)[None, :] < jnp.arange(m)[:, None]).astype(BF16)
    ltri_p, ltri_s = ltri(tm_moe), ltri(n_s_pad)
    rw = jnp.pad(router_w, ((0, 0), (0, LANE - N_EXPERTS))).astype(BF16)
    rb = jnp.pad(router_b, (0, LANE - N_EXPERTS)).reshape(1, LANE)
    conv0_p = jnp.zeros((b, SUBLANE, D_RNN), F32)
    h0_p = jnp.zeros((b, SUBLANE, D_RNN), F32)
    pad_rows = SUBLANE - (CONV_WIDTH - 1)
    n_pages = page_table.shape[1]
    pt_flat = page_table.reshape(-1)
    cache = cache_kv.reshape(DEPTH, cache_kv.shape[1], PAGE_SIZE, 2 * HALF)
    cwin = cache_win.reshape(DEPTH, db, cache_win.shape[2], HALF)
    expand_s = (jnp.arange(ROWS_PER_STEP)[None, :] // BLOCK == jnp.arange(LANE)[:, None]).astype(BF16)

    outs = [[] for _ in range(8)]
    for l in range(DEPTH):
        c_gate = 1024 + 6 * KV_HEADS * HEAD_DIM
        w_main = jnp.concatenate([w_in[l, :, :c_gate], w_in[l, :, c_gate + 3 * N_HEADS:]], axis=1).astype(BF16)
        w_gate = _gate_columns(w_in[l, :, c_gate:c_gate + 3 * N_HEADS]).astype(BF16)
        w1 = cmp_w1[l].astype(BF16)
        w2 = cmp_w2[l].astype(BF16)
        w_ai = jnp.concatenate([lru_w_a[l], lru_w_i[l]], axis=-1).astype(BF16)
        vec = lambda v: v.reshape(1, -1)
        lru = (conv_w[l], vec(conv_b[l]), w_ai, vec(lru_b_a[l]), vec(lru_b_i[l]), vec(lru_lam[l]))
        wo = w_out[l].astype(BF16)
        wg, wu, wd = exp_w_gate[l].astype(BF16), exp_w_up[l].astype(BF16), exp_w_down[l].astype(BF16)
        gn = vec(group_norm[l])

        q, kv4, kwin, kvb, gates, u, gg = _in_proj(xp, w_main, w_gate, tm_p)
        kvc = _compress(kv4, cmp_pe[l], w1, w2, b, t)
        attn = _prompt_attention(q, gates, kvc, kvb, expand, b, t, tq, tk)
        rnn, hl = _rglru(u, gg, conv0_p, h0_p, *lru, b, t, tt)
        x1, x1b, rt = _mixer(attn, rnn, xp, gn, wo, vec(ln_g[l, 0]), vec(ln_b[l, 0]), rw, rb, tm_p)
        y = _moe(x1b, rt, ltri_p, wg, wu, wd, tm_moe)
        xp = _ln2(x1, y, vec(ln_g[l, 1]), vec(ln_b[l, 1]), tm_p)
        outs[0].append(kv4.reshape(b, t, KV_SLOTS, KV_HEADS, HEAD_DIM))
        outs[2].append(kwin.reshape(b, t, 2, KV_HEADS, HEAD_DIM)[:, -min(WINDOW, t):])
        outs[4].append(u.reshape(b, t, D_RNN)[:, -(CONV_WIDTH - 1):])
        outs[6].append(hl.reshape(b, D_RNN))

        q, kv4, kwin, kvb, gates, u, gg = _in_proj(xs, w_main, w_gate, n_s)
        kvc = _sample_compress(pt_flat, cache, l, cmp_pe[l], w1.reshape(2, BLOCK // 2, 2 * HEAD_DIM, HEAD_DIM),
                               w2, db, n_pages)
        attn, wnew = _sample_attention(pt_flat, cache, l, q.astype(F32), gates, kvc, kv4, kwin, cwin,
                                       expand_s, db, ds, n_pages)
        conv0_s = jnp.pad(state_conv[l], ((0, 0), (pad_rows, 0), (0, 0)))
        h0_s = jnp.pad(state_h[l][:, None, :], ((0, 0), (0, SUBLANE - 1), (0, 0)))
        rnn, hl = _rglru(u, gg, conv0_s, h0_s, *lru, db, ds, ds)
        x1, x1b, rt = _mixer(attn, rnn, xs, gn, wo, vec(ln_g[l, 0]), vec(ln_b[l, 0]),
                        rw, rb, n_s)
        x1p = jnp.pad(x1b, ((0, n_s_pad - n_s), (0, 0)))
        rtp = jnp.pad(rt, ((0, n_s_pad - n_s), (0, 0)), constant_values=-1.0)
        y = _moe_single_tile(x1p, rtp, ltri_s, wg, wu, wd)[:n_s]
        xs = _ln2(x1, y, vec(ln_g[l, 1]), vec(ln_b[l, 1]), n_s)
        outs[1].append(kv4.reshape(db, ds, KV_SLOTS, KV_HEADS, HEAD_DIM))
        outs[3].append(wnew.reshape(db, WINDOW, 2, KV_HEADS, HEAD_DIM))
        xpc = jnp.concatenate([state_conv[l], u.reshape(db, ds, D_RNN)], axis=1)
        outs[5].append(xpc[:, -(CONV_WIDTH - 1):])
        outs[7].append(hl.reshape(db, D_RNN))

    st = [jnp.stack(o) for o in outs]
    return (xp.reshape(b, t, D_MODEL), xs.reshape(db, ds, D_MODEL),
            st[0], st[1], st[2], st[3], st[4], st[5], st[6], st[7])
```

```python
import functools

import jax
import jax.numpy as jnp
from jax import lax
from jax.experimental import pallas as pl
from jax.experimental.pallas import tpu as pltpu

F32 = jnp.float32
BF16 = jnp.bfloat16
I32 = jnp.int32

D_MODEL = 2048
DEPTH = 4
PAGE_SIZE = 128
D_ATTN = D_MODEL // 2
D_RNN = D_MODEL - D_ATTN
N_HEADS = 8
HEAD_DIM = D_ATTN // N_HEADS
KV_HEADS = 2
Q_PER_KV = N_HEADS // KV_HEADS
BLOCK = 64
N_SELECT = 16
WINDOW = 512
FORCED_SCORE = 1e4
KV_SLOTS = 4
CONV_WIDTH = 4
LRU_BLOCKS = 8
LRU_BW = D_RNN // LRU_BLOCKS
LRU_C = 8.0
N_EXPERTS = 16
N_GROUPS = 4
EXPERTS_PER_GROUP = N_EXPERTS // N_GROUPS
D_EXPERT = 512
ALPHA = (2 * DEPTH) ** 0.25
LN_EPS = 1e-5
SCALE = HEAD_DIM ** -0.5
Q_SCALE = SCALE * 1.4426950408889634
NEG = -1e30

LANE = 128
SUBLANE = 8
MOE_CHUNK = 128
VMEM_LIMIT = 56 * 1024 * 1024

_NT = (((1,), (1,)), ((), ()))
_TN = (((0,), (0,)), ((), ()))


def _params(n_axes):
    return pltpu.CompilerParams(dimension_semantics=("arbitrary",) * n_axes,
                                vmem_limit_bytes=VMEM_LIMIT)


def _sigmoid(x):
    return 1.0 / (1.0 + jnp.exp(-x))


def _gelu(x):
    return 0.5 * x * (1.0 + jnp.tanh(0.7978845608028654 * (x + 0.044715 * (x * x * x))))


def _dot(a, b):
    return jnp.dot(a, b, preferred_element_type=F32)


def _resident(shape):
    nd = len(shape)
    return pl.BlockSpec(shape, lambda *_: (0,) * nd, pipeline_mode=pl.Buffered(1))


def _in_proj_kernel(x_ref, w_ref, wg_ref, q_ref, kv4_ref, kwin_ref, kvb_ref, gates_ref, u_ref, gg_ref):
    xb = x_ref[...].astype(BF16)

    def mm(lo, hi):
        return _dot(xb, w_ref[:, lo:hi])

    q_ref[...] = (mm(0, 1024) * Q_SCALE).astype(BF16)
    kv = mm(1024, 2048)
    kv4_ref[...] = kv
    kw = mm(2048, 2560)
    kwin_ref[...] = kw
    kvb_ref[:, 0:512] = kv[:, 512:1024].astype(BF16)
    kvb_ref[:, 512:1024] = kw.astype(BF16)
    gates_ref[...] = _sigmoid(_dot(xb, wg_ref[...]))
    u_ref[...] = mm(2560, 3584)
    gg_ref[...] = mm(3584, 4608)


def _in_proj(x, w_main, w_gate, tm):
    n = x.shape[0]
    row = lambda c: pl.BlockSpec((tm, c), lambda i: (i, 0))
    return pl.pallas_call(
        _in_proj_kernel,
        grid=(n // tm,),
        in_specs=[row(D_MODEL), _resident(w_main.shape), _resident(w_gate.shape)],
        out_specs=[row(1024), row(1024), row(512), row(1024), row(256), row(1024), row(1024)],
        out_shape=[jax.ShapeDtypeStruct((n, 1024), BF16),
                   jax.ShapeDtypeStruct((n, 1024), F32),
                   jax.ShapeDtypeStruct((n, 512), F32),
                   jax.ShapeDtypeStruct((n, 1024), BF16),
                   jax.ShapeDtypeStruct((n, 256), F32),
                   jax.ShapeDtypeStruct((n, 1024), F32),
                   jax.ShapeDtypeStruct((n, 1024), F32)],
        compiler_params=_params(1),
        name="in_proj",
    )(x, w_main, w_gate)


def _compress_kernel(x0_ref, x1_ref, x2_ref, x3_ref, pe_ref, w1_ref, w2_ref, o_ref, *, nblk):
    x_refs = (x0_ref, x1_ref, x2_ref, x3_ref)
    for s in range(2):
        acc = [jnp.zeros((nblk, HEAD_DIM), F32) for _ in range(KV_HEADS)]
        for l in range(BLOCK):
            pe = pe_ref[s, l:l + 1, :]
            w = w1_ref[s, l]
            for g in range(KV_HEADS):
                rows = x_refs[s * KV_HEADS + g][pl.ds(l, nblk, stride=BLOCK), :]
                acc[g] = acc[g] + _dot((rows + pe).astype(BF16), w)
        for g in range(KV_HEADS):
            h = _gelu(acc[g]).astype(BF16)
            o_ref[s * KV_HEADS + g] = _dot(h, w2_ref[s]).astype(BF16)


def _compress(kv4, pe, w1, w2, batch, t_len):
    nblk = t_len // BLOCK
    col = lambda c: pl.BlockSpec((t_len, HEAD_DIM), lambda b: (b, c))
    return pl.pallas_call(
        functools.partial(_compress_kernel, nblk=nblk),
        grid=(batch,),
        in_specs=[col(0), col(1), col(2), col(3),
                  _resident(pe.shape), _resident(w1.shape), _resident(w2.shape)],
        out_specs=pl.BlockSpec((None, 4, nblk, HEAD_DIM), lambda b: (b, 0, 0, 0)),
        out_shape=jax.ShapeDtypeStruct((batch, 4, nblk, HEAD_DIM), BF16),
        compiler_params=_params(1),
        name="compress",
    )(kv4, kv4, kv4, kv4, pe, w1, w2)


def _attn_kernel(q_ref, gates_ref, kc_ref, vc_ref, ks_ref, vs_ref, kw_ref, vw_ref, e_ref, o_ref,
                 *, tq, tk, t_len):
    i = pl.program_id(2)
    s0 = i * tq
    nblk = t_len // BLOCK
    qpos = s0 + lax.broadcasted_iota(I32, (tq, 1), 0)
    qh = [q_ref[:, r * HEAD_DIM:(r + 1) * HEAD_DIM] for r in range(Q_PER_KV)]

    blk = lax.broadcasted_iota(I32, (nblk, tq), 0)
    qpos_l = s0 + lax.broadcasted_iota(I32, (1, tq), 1)
    cmask = (blk + 1) * BLOCK - 1 <= qpos_l
    kc = kc_ref[...]
    vc = vc_ref[...]
    imp = jnp.zeros((nblk, tq), F32)
    o_cmp = []
    for r in range(Q_PER_KV):
        sc = jnp.where(cmask, lax.dot_general(kc, qh[r], _NT, preferred_element_type=F32), NEG)
        m = jnp.max(sc, axis=0, keepdims=True)
        e = jnp.where(cmask, jnp.exp2(sc - m), 0.0)
        p = e / jnp.maximum(jnp.sum(e, axis=0, keepdims=True), 1e-30)
        imp = imp + p
        o_cmp.append(lax.dot_general(p.astype(BF16), vc, _TN, preferred_element_type=F32))

    cur = jnp.right_shift(qpos_l, 6)
    forced = (blk == 0) | (blk == cur) | (blk == cur - 1)
    valid = blk <= cur
    score = jnp.where(valid, jnp.where(forced, FORCED_SCORE, imp), -1.0)
    rank = jnp.zeros((nblk, tq), I32)
    for j in range(nblk):
        cj = score[j:j + 1, :]
        beats = (cj > score) | ((cj == score) & (blk > j))
        rank = rank + beats.astype(I32)
    sel = ((rank < min(N_SELECT, nblk)) & valid).astype(BF16)

    def sel_body(kt, carry):
        off = pl.multiple_of(kt * tk, tk)
        k = ks_ref[pl.ds(off, tk), :]
        v = vs_ref[pl.ds(off, tk), :]
        hit = lax.dot_general(sel, e_ref[:, pl.ds(off, tk)], _TN, preferred_element_type=F32)
        kpos = off + lax.broadcasted_iota(I32, (tq, tk), 1)
        bias = jnp.where((hit > 0.5) & (kpos <= qpos), 0.0, NEG)
        out = []
        for r in range(Q_PER_KV):
            m, l, acc = carry[r]
            s = lax.dot_general(qh[r], k, _NT, preferred_element_type=F32) + bias
            m_new = jnp.maximum(m, jnp.max(s, axis=1, keepdims=True))
            a = jnp.exp2(m - m_new)
            p = jnp.exp2(s - m_new)
            l = a * l + jnp.sum(p, axis=1, keepdims=True)
            acc = a * acc + _dot(p.astype(BF16), v)
            out.append((m_new, l, acc))
        return tuple(out)

    init = tuple((jnp.full((tq, 1), NEG, F32), jnp.zeros((tq, 1), F32), jnp.zeros((tq, HEAD_DIM), F32))
                 for _ in range(Q_PER_KV))
    n_t = (s0 + tq + tk - 1) // tk
    fin = lax.fori_loop(0, n_t, sel_body, init)
    o_sel = [acc / l for (_, l, acc) in fin]

    wlen = WINDOW + tq
    start = pl.multiple_of(jnp.maximum(s0 - WINDOW, 0), tq)
    kw = kw_ref[pl.ds(start, wlen), :]
    vw = vw_ref[pl.ds(start, wlen), :]
    kposw = start + lax.broadcasted_iota(I32, (tq, wlen), 1)
    biasw = jnp.where((kposw <= qpos) & (kposw > qpos - WINDOW), 0.0, NEG)

    for r in range(Q_PER_KV):
        s = lax.dot_general(qh[r], kw, _NT, preferred_element_type=F32) + biasw
        p = jnp.exp2(s - jnp.max(s, axis=1, keepdims=True))
        o_win = _dot(p.astype(BF16), vw) / jnp.sum(p, axis=1, keepdims=True)
        g_cmp = gates_ref[:, r:r + 1]
        g_sel = gates_ref[:, Q_PER_KV + r:Q_PER_KV + r + 1]
        g_win = gates_ref[:, 2 * Q_PER_KV + r:2 * Q_PER_KV + r + 1]
        o_ref[:, r * HEAD_DIM:(r + 1) * HEAD_DIM] = g_cmp * o_cmp[r] + g_sel * o_sel[r] + g_win * o_win


def _prompt_attention(q, gates, kvc, kvb, expand, batch, t_len, tq, tk):
    nq = t_len // tq
    gw = Q_PER_KV * HEAD_DIM
    col = lambda c: pl.BlockSpec((t_len, HEAD_DIM), lambda b, g, i: (b, c + g))
    cmp_blk = lambda c: pl.BlockSpec((None, None, t_len // BLOCK, HEAD_DIM), lambda b, g, i: (b, c + g, 0, 0))
    return pl.pallas_call(
        functools.partial(_attn_kernel, tq=tq, tk=tk, t_len=t_len),
        grid=(batch, KV_HEADS, nq),
        in_specs=[pl.BlockSpec((tq, gw), lambda b, g, i: (b * nq + i, g)),
                  pl.BlockSpec((tq, LANE), lambda b, g, i: (b * nq + i, g)),
                  cmp_blk(0), cmp_blk(2),
                  col(0), col(2), col(4), col(6),
                  pl.BlockSpec(expand.shape, lambda b, g, i: (0, 0))],
        out_specs=pl.BlockSpec((tq, gw), lambda b, g, i: (b * nq + i, g)),
        out_shape=jax.ShapeDtypeStruct((batch * t_len, D_ATTN), F32),
        compiler_params=_params(3),
        name="prompt_attention",
    )(q, gates, kvc, kvc, kvb, kvb, kvb, kvb, expand)


def _rglru_kernel(u_ref, gg_ref, c0_ref, h0_ref, cw_ref, cb_ref, wai_ref, ba_ref, bi_ref, lam_ref,
                  y_ref, hl_ref, ext_ref, a_ref, b_ref, tail_ref, h_ref, *, tt):
    t = pl.program_id(1)

    @pl.when(t == 0)
    def _():
        tail_ref[...] = c0_ref[...]
        h_ref[...] = h0_ref[...]

    u = u_ref[...]
    ext_ref[0:SUBLANE, :] = tail_ref[...]
    ext_ref[SUBLANE:SUBLANE + tt, :] = u
    base = SUBLANE - (CONV_WIDTH - 1)
    xc = cb_ref[...] + ext_ref[base:base + tt, :] * cw_ref[0:1, :]
    for j in range(1, CONV_WIDTH - 1):
        xc = xc + ext_ref[base + j:base + j + tt, :] * cw_ref[j:j + 1, :]
    xc = xc + u * cw_ref[CONV_WIDTH - 1:CONV_WIDTH, :]
    tail_ref[...] = ext_ref[tt:tt + SUBLANE, :]

    xcb = xc.astype(BF16)
    for n in range(LRU_BLOCKS):
        c = slice(n * LRU_BW, (n + 1) * LRU_BW)
        ai = _dot(xcb[:, c], wai_ref[n])
        r = _sigmoid(ai[:, :LRU_BW] + ba_ref[:, c])
        ig = _sigmoid(ai[:, LRU_BW:] + bi_ref[:, c])
        nl = -lam_ref[:, c]
        softplus = jnp.maximum(nl, 0.0) + jnp.log1p(jnp.exp(-jnp.abs(nl)))
        log_a = (-LRU_C * softplus) * r
        a = jnp.exp(log_a)
        gain = jnp.sqrt(-jnp.tanh(log_a) * (a * a + 1.0))
        a_ref[:, c] = a
        b_ref[:, c] = gain * (ig * xc[:, c])

    def group(gi, h):
        off = pl.multiple_of(gi * SUBLANE, SUBLANE)
        a8 = a_ref[pl.ds(off, SUBLANE), :]
        b8 = b_ref[pl.ds(off, SUBLANE), :]
        rows = []
        for r in range(SUBLANE):
            h = a8[r:r + 1, :] * h + b8[r:r + 1, :]
            rows.append(h)
        a_ref[pl.ds(off, SUBLANE), :] = jnp.concatenate(rows, axis=0)
        return h

    h = lax.fori_loop(0, tt // SUBLANE, group, h_ref[0:1, :])
    h_ref[0:1, :] = h
    hl_ref[...] = h
    y_ref[...] = a_ref[...] * _gelu(gg_ref[...])


def _rglru(u, gg, conv0, h0, conv_w, conv_b, w_ai, b_a, b_i, lam, batch, t_len, tt):
    nt = t_len // tt
    c = D_RNN
    tile = pl.BlockSpec((tt, c), lambda b, t: (b * nt + t, 0))
    vec = _resident((1, c))
    return pl.pallas_call(
        functools.partial(_rglru_kernel, tt=tt),
        grid=(batch, nt),
        in_specs=[tile, tile,
                  pl.BlockSpec((None, SUBLANE, c), lambda b, t: (b, 0, 0)),
                  pl.BlockSpec((None, SUBLANE, c), lambda b, t: (b, 0, 0)),
                  _resident(conv_w.shape), vec, _resident(w_ai.shape), vec, vec, vec],
        out_specs=[tile, pl.BlockSpec((None, 1, c), lambda b, t: (b, 0, 0))],
        out_shape=[jax.ShapeDtypeStruct((batch * t_len, c), F32),
                   jax.ShapeDtypeStruct((batch, 1, c), F32)],
        scratch_shapes=[pltpu.VMEM((tt + SUBLANE, c), F32), pltpu.VMEM((tt, c), F32),
                        pltpu.VMEM((tt, c), F32), pltpu.VMEM((SUBLANE, c), F32),
                        pltpu.VMEM((SUBLANE, c), F32)],
        compiler_params=_params(2),
        name="rglru",
    )(u, gg, conv0, h0, conv_w, conv_b, w_ai, b_a, b_i, lam)


def _layer_norm(z, g, b):
    zc = z - jnp.mean(z, axis=1, keepdims=True)
    var = jnp.mean(zc * zc, axis=1, keepdims=True)
    return zc * lax.rsqrt(var + LN_EPS) * g + b


def _mixer_kernel(attn_ref, rnn_ref, x_ref, gn_ref, wo_ref, lng_ref, lnb_ref, rw_ref, rb_ref,
                  x1_ref, x1b_ref, rt_ref):
    def rms(v, g):
        return (v * lax.rsqrt(jnp.mean(v * v, axis=1, keepdims=True) + LN_EPS) * g).astype(BF16)

    an = rms(attn_ref[...], gn_ref[:, 0:D_ATTN])
    rn = rms(rnn_ref[...], gn_ref[:, D_ATTN:D_MODEL])
    y = _dot(an, wo_ref[0:D_ATTN, :]) + _dot(rn, wo_ref[D_ATTN:D_MODEL, :])
    x1 = _layer_norm(ALPHA * x_ref[...] + y, lng_ref[...], lnb_ref[...])
    x1_ref[...] = x1
    x1b = x1.astype(BF16)
    x1b_ref[...] = x1b

    s = _sigmoid(_dot(x1b, rw_ref[...]))
    biased = s + rb_ref[...]
    sc = [s[:, e:e + 1] for e in range(N_EXPERTS)]
    bc = [biased[:, e:e + 1] for e in range(N_EXPERTS)]
    best = grp = None
    for g in range(N_GROUPS):
        v = bc[g * EXPERTS_PER_GROUP:(g + 1) * EXPERTS_PER_GROUP]
        top2 = None
        for i in range(EXPERTS_PER_GROUP):
            for j in range(i + 1, EXPERTS_PER_GROUP):
                pair = v[i] + v[j]
                top2 = pair if top2 is None else jnp.maximum(top2, pair)
        if g == 0:
            best, grp = top2, jnp.zeros_like(top2, dtype=I32)
        else:
            better = top2 > best
            grp = jnp.where(better, g, grp)
            best = jnp.where(better, top2, best)

    def in_group(cols, j):
        out = cols[j]
        for g in range(1, N_GROUPS):
            out = jnp.where(grp == g, cols[g * EXPERTS_PER_GROUP + j], out)
        return out

    bv = [in_group(bc, j) for j in range(EXPERTS_PER_GROUP)]
    sv = [in_group(sc, j) for j in range(EXPERTS_PER_GROUP)]

    def argmax_first(vals, skip=None):
        bestv = idx = None
        for j, v in enumerate(vals):
            vj = v if skip is None else jnp.where(skip == j, -jnp.inf, v)
            if j == 0:
                bestv, idx = vj, jnp.zeros_like(grp)
            else:
                better = vj > bestv
                idx = jnp.where(better, j, idx)
                bestv = jnp.where(better, vj, bestv)
        return idx

    i1 = argmax_first(bv)
    i2 = argmax_first(bv, skip=i1)

    def pick(vals, idx):
        out = vals[0]
        for j in range(1, len(vals)):
            out = jnp.where(idx == j, vals[j], out)
        return out

    s1 = pick(sv, i1)
    s2 = pick(sv, i2)
    den = s1 + s2
    w1 = s1 / den
    w2 = s2 / den
    lane = lax.broadcasted_iota(I32, rt_ref.shape, 1)
    rt = jnp.where(lane == EXPERTS_PER_GROUP, grp.astype(F32), 0.0)
    rt = rt + jnp.where(lane == i1, w1, 0.0) + jnp.where(lane == i2, w2, 0.0)
    rt_ref[...] = rt


def _mixer(attn, rnn, x, gn, w_out, ln_g, ln_b, rw, rb, tm):
    n = x.shape[0]
    row = lambda c: pl.BlockSpec((tm, c), lambda i: (i, 0))
    vec = _resident((1, D_MODEL))
    return pl.pallas_call(
        _mixer_kernel,
        grid=(n // tm,),
        in_specs=[row(D_ATTN), row(D_RNN), row(D_MODEL), vec, _resident(w_out.shape), vec, vec,
                  _resident(rw.shape), _resident(rb.shape)],
        out_specs=[row(D_MODEL), row(D_MODEL), row(LANE)],
        out_shape=[jax.ShapeDtypeStruct((n, D_MODEL), F32),
                   jax.ShapeDtypeStruct((n, D_MODEL), BF16),
                   jax.ShapeDtypeStruct((n, LANE), F32)],
        compiler_params=_params(1),
        name="mixer_out",
    )(attn, rnn, x, gn, w_out, ln_g, ln_b, rw, rb)


def _split3(v):
    a = v.astype(BF16)
    r = v - a.astype(F32)
    b = r.astype(BF16)
    c = (r - b.astype(F32)).astype(BF16)
    return a, b, c


def _moe_kernel(*refs, tm, group, accumulate):
    if accumulate:
        x_ref, rt_ref, acc_ref, ltri_ref, wg_ref, wu_ref, wd_ref, o_ref = refs
    else:
        x_ref, rt_ref, ltri_ref, wg_ref, wu_ref, wd_ref, o_ref = refs
    g = pl.program_id(1) if group is None else group
    rt = rt_ref[...]
    member = rt[:, EXPERTS_PER_GROUP:EXPERTS_PER_GROUP + 1] == jnp.asarray(g, F32)
    mf = member.astype(F32)
    pos = _dot(ltri_ref[...], jnp.broadcast_to(mf, (tm, LANE)).astype(BF16))[:, 0:1].astype(I32)
    count = jnp.sum(mf).astype(I32)
    xb = x_ref[...]
    c1, c2, c3 = _split3(rt)
    lane = lax.broadcasted_iota(I32, (tm, MOE_CHUNK), 1)
    if accumulate:
        o_ref[...] = acc_ref[...]
    elif group is None:
        @pl.when(g == 0)
        def _():
            o_ref[...] = jnp.zeros_like(o_ref)
    else:
        o_ref[...] = jnp.zeros_like(o_ref)

    def chunk(k, carry):
        ptf = jnp.where(member & (pos - k * MOE_CHUNK == lane), 1.0, 0.0)
        pt = ptf.astype(BF16)
        p = ptf.T.astype(BF16)
        xc = _dot(p, xb).astype(BF16)
        cw = _dot(p, c1) + _dot(p, c2) + _dot(p, c3)
        y = jnp.zeros((MOE_CHUNK, D_MODEL), F32)
        for j in range(EXPERTS_PER_GROUP):
            hg = _dot(xc, wg_ref[j])
            hu = _dot(xc, wu_ref[j])
            h = (hg * _sigmoid(hg)) * hu * cw[:, j:j + 1]
            y = y + _dot(h.astype(BF16), wd_ref[j])
        yh = y.astype(BF16)
        yl = (y - yh.astype(F32)).astype(BF16)
        o_ref[...] += _dot(pt, yh) + _dot(pt, yl)
        return carry

    lax.fori_loop(0, (count + MOE_CHUNK - 1) // MOE_CHUNK, chunk, 0)


def _moe_single_tile(x, rt, ltri, wg, wu, wd):
    n = x.shape[0]
    row = lambda c: pl.BlockSpec((n, c), lambda t, g: (0, 0))
    grp = lambda a, b: pl.BlockSpec((EXPERTS_PER_GROUP, a, b), lambda t, g: (g, 0, 0))
    return pl.pallas_call(
        functools.partial(_moe_kernel, tm=n, group=None, accumulate=False),
        grid=(1, N_GROUPS),
        in_specs=[row(D_MODEL), row(LANE), pl.BlockSpec(ltri.shape, lambda t, g: (0, 0)),
                  grp(D_MODEL, D_EXPERT), grp(D_MODEL, D_EXPERT), grp(D_EXPERT, D_MODEL)],
        out_specs=row(D_MODEL),
        out_shape=jax.ShapeDtypeStruct((n, D_MODEL), F32),
        compiler_params=_params(2),
        name="grouped_moe_single_tile",
    )(x, rt, ltri, wg, wu, wd)


def _moe(x, rt, ltri, wg, wu, wd, tm):
    n = x.shape[0]
    row = lambda c: pl.BlockSpec((tm, c), lambda t: (t, 0))
    acc = None
    for g in range(N_GROUPS):
        grp = lambda a, b: pl.BlockSpec((EXPERTS_PER_GROUP, a, b), lambda t, g=g: (g, 0, 0),
                                        pipeline_mode=pl.Buffered(1))
        first = acc is None
        acc = pl.pallas_call(
            functools.partial(_moe_kernel, tm=tm, group=g, accumulate=not first),
            grid=(n // tm,),
            in_specs=[row(D_MODEL), row(LANE)] + ([] if first else [row(D_MODEL)]) + [
                _resident(ltri.shape), grp(D_MODEL, D_EXPERT), grp(D_MODEL, D_EXPERT), grp(D_EXPERT, D_MODEL)],
            out_specs=row(D_MODEL),
            out_shape=jax.ShapeDtypeStruct((n, D_MODEL), F32),
            input_output_aliases={} if first else {2: 0},
            compiler_params=_params(1),
            name="grouped_moe",
        )(*((x, rt) + (() if first else (acc,)) + (ltri, wg, wu, wd)))
    return acc


def _ln2_kernel(x_ref, y_ref, g_ref, b_ref, o_ref):
    o_ref[...] = _layer_norm(ALPHA * x_ref[...] + y_ref[...], g_ref[...], b_ref[...])


def _ln2(x, y, g, b, tm):
    n = x.shape[0]
    row = pl.BlockSpec((tm, D_MODEL), lambda i: (i, 0))
    vec = _resident((1, D_MODEL))
    return pl.pallas_call(
        _ln2_kernel, grid=(n // tm,), in_specs=[row, row, vec, vec], out_specs=row,
        out_shape=jax.ShapeDtypeStruct((n, D_MODEL), F32),
        compiler_params=_params(1), name="moe_residual_ln",
    )(x, y, g, b)


PAGES_PER_STEP = 32
ROWS_PER_STEP = PAGES_PER_STEP * PAGE_SIZE
BLOCKS_PER_STEP = ROWS_PER_STEP // BLOCK
HALF = KV_HEADS * HEAD_DIM * 2
ROW_SLABS = KV_SLOTS * KV_HEADS
WIN_SLABS = 2 * KV_HEADS


def _page_specs(layer, n_pages):
    def spec(j):
        return pl.BlockSpec((None, None, PAGE_SIZE * ROW_SLABS, HEAD_DIM),
                            lambda b, s, pt: (layer, pt[b * n_pages + s * PAGES_PER_STEP + j], 0, 0))
    return [spec(j) for j in range(PAGES_PER_STEP)]


def _slab(page, c):
    return page[pl.ds(c, PAGE_SIZE, stride=ROW_SLABS), :]


def _sample_compress_kernel(pt_ref, *refs):
    pages = refs[:PAGES_PER_STEP]
    pe_ref, w1_ref, w2_ref, o_ref, rows_ref = refs[PAGES_PER_STEP:]
    for j, page in enumerate(pages):
        for c in range(2 * KV_HEADS):
            rows_ref[c, j * PAGE_SIZE:(j + 1) * PAGE_SIZE, :] = _slab(page, c)
    for s in range(2):
        acc = jnp.zeros((KV_HEADS * BLOCKS_PER_STEP, HEAD_DIM), F32)
        for lp in range(BLOCK // 2):
            parts = []
            for l in (2 * lp, 2 * lp + 1):
                pe = pe_ref[s, l:l + 1, :]
                parts.append(jnp.concatenate(
                    [rows_ref[s * KV_HEADS + g, pl.ds(l, BLOCKS_PER_STEP, stride=BLOCK), :] + pe
                     for g in range(KV_HEADS)], axis=0))
            lhs = jnp.concatenate(parts, axis=1).astype(BF16)
            acc = acc + _dot(lhs, w1_ref[s, lp])
        out = _dot(_gelu(acc).astype(BF16), w2_ref[s]).astype(BF16)
        for g in range(KV_HEADS):
            o_ref[s * KV_HEADS + g] = out[g * BLOCKS_PER_STEP:(g + 1) * BLOCKS_PER_STEP]


def _sample_compress(page_table, cache, layer, pe, w1_pairs, w2, db, n_pages):
    n_steps = n_pages // PAGES_PER_STEP
    nblk = n_pages * PAGE_SIZE // BLOCK
    const = lambda shape: pl.BlockSpec(shape, lambda b, s, pt: (0,) * len(shape))
    return pl.pallas_call(
        _sample_compress_kernel,
        grid_spec=pltpu.PrefetchScalarGridSpec(
            num_scalar_prefetch=1, grid=(db, n_steps),
            in_specs=_page_specs(layer, n_pages) + [const(pe.shape), const(w1_pairs.shape), const(w2.shape)],
            out_specs=pl.BlockSpec((None, 4, BLOCKS_PER_STEP, HEAD_DIM), lambda b, s, pt: (b, 0, s, 0)),
            scratch_shapes=[pltpu.VMEM((2 * KV_HEADS, ROWS_PER_STEP, HEAD_DIM), F32)]),
        out_shape=jax.ShapeDtypeStruct((db, 4, nblk, HEAD_DIM), BF16),
        compiler_params=_params(2),
        name="sample_compress",
    )(page_table, *([cache] * PAGES_PER_STEP), pe, w1_pairs, w2)


def _softmax_update(s, v, m, l, acc):
    m_new = jnp.maximum(m, jnp.max(s, axis=1, keepdims=True))
    a = jnp.exp2(m - m_new)
    p = jnp.exp2(s - m_new)
    return m_new, a * l + jnp.sum(p, axis=1, keepdims=True), a * acc + _dot(p.astype(BF16), v)


def _sample_attn_kernel(pt_ref, *refs, ds, past_len, n_steps):
    pages = refs[:PAGES_PER_STEP]
    (q_ref, gates_ref, kvc_ref, new_ref, nwin_ref, nwin4_ref, cwin_ref, e_ref,
     o_ref, wout_ref, k_ref, v_ref, sel_ref, ocmp_ref, m_ref, l_ref, acc_ref) = refs[PAGES_PER_STEP:]
    step = pl.program_id(1)
    rows = Q_PER_KV * ds
    nc = past_len // BLOCK
    ns = -(-(past_len + ds) // BLOCK)
    ns_pad = -(-ns // LANE) * LANE
    qi = lax.broadcasted_iota(I32, (rows, 1), 0) % ds
    qpos = past_len + qi

    def q_group(g):
        return jnp.concatenate([q_ref[:, (g * Q_PER_KV + r) * HEAD_DIM:(g * Q_PER_KV + r + 1) * HEAD_DIM]
                                for r in range(Q_PER_KV)], axis=0).astype(BF16)

    @pl.when(step == 0)
    def _():
        blk_c = lax.broadcasted_iota(I32, (rows, nc), 1)
        cmask = (blk_c + 1) * BLOCK - 1 <= qpos
        blk = lax.broadcasted_iota(I32, (ds, ns_pad), 1)
        qpos_q = past_len + lax.broadcasted_iota(I32, (ds, 1), 0)
        cur = qpos_q // BLOCK
        forced = (blk == 0) | (blk == cur) | (blk == cur - 1)
        valid = (blk <= cur) & (blk < ns)
        for g in range(KV_HEADS):
            qg = q_group(g)
            sc = jnp.where(cmask, lax.dot_general(qg, kvc_ref[g], _NT, preferred_element_type=F32), NEG)
            m = jnp.max(sc, axis=1, keepdims=True)
            e = jnp.where(cmask, jnp.exp2(sc - m), 0.0)
            p = e / jnp.maximum(jnp.sum(e, axis=1, keepdims=True), 1e-30)
            ocmp_ref[g] = _dot(p.astype(BF16), kvc_ref[KV_HEADS + g])
            imp = p[0:ds]
            for r in range(1, Q_PER_KV):
                imp = imp + p[r * ds:(r + 1) * ds]
            imp = jnp.concatenate([imp, jnp.zeros((ds, ns_pad - nc), F32)], axis=1)
            score = jnp.where(valid, jnp.where(forced, FORCED_SCORE, imp), -1.0)
            rank = jnp.zeros((ds, ns_pad), I32)
            for j in range(ns):
                cj = score[:, j:j + 1]
                rank = rank + ((cj > score) | ((cj == score) & (blk > j))).astype(I32)
            sel = ((rank < min(N_SELECT, ns)) & valid).astype(F32)
            sel = jnp.concatenate([sel] * Q_PER_KV, axis=0)
            for t in range(n_steps):
                part = sel[:, t * BLOCKS_PER_STEP:(t + 1) * BLOCKS_PER_STEP]
                sel_ref[g, t] = jnp.concatenate(
                    [part, jnp.zeros((rows, LANE - BLOCKS_PER_STEP), F32)], axis=1).astype(BF16)
            sel_ref[g, n_steps] = jnp.broadcast_to(sel[:, nc:nc + 1], (rows, LANE)).astype(BF16)
            m_ref[g] = jnp.full((rows, 1), NEG, F32)
            l_ref[g] = jnp.zeros((rows, 1), F32)
            acc_ref[g] = jnp.zeros((rows, HEAD_DIM), F32)

    for j, page in enumerate(pages):
        for g in range(KV_HEADS):
            k_ref[g, j * PAGE_SIZE:(j + 1) * PAGE_SIZE, :] = _slab(page, 2 * KV_HEADS + g).astype(BF16)
            v_ref[g, j * PAGE_SIZE:(j + 1) * PAGE_SIZE, :] = _slab(page, 3 * KV_HEADS + g).astype(BF16)
    for g in range(KV_HEADS):
        qg = q_group(g)
        hit = _dot(sel_ref[g, step], e_ref[...])
        s = lax.dot_general(qg, k_ref[g], _NT, preferred_element_type=F32)
        s = jnp.where(hit > 0.5, s, NEG)
        m, l, acc = _softmax_update(s, v_ref[g], m_ref[g], l_ref[g], acc_ref[g])
        m_ref[g] = m
        l_ref[g] = l
        acc_ref[g] = acc

    @pl.when(step == n_steps - 1)
    def _():
        pad = jnp.zeros((2 * SUBLANE - ds, HEAD_DIM), F32)
        col = lax.broadcasted_iota(I32, (rows, 2 * SUBLANE), 1)
        new_ok = (col <= qi) & (col < ds)
        wpos = lax.broadcasted_iota(I32, (rows, WINDOW), 1)
        for g in range(KV_HEADS):
            qg = q_group(g)
            kn = jnp.concatenate([new_ref[:, (2 * KV_HEADS + g) * HEAD_DIM:(2 * KV_HEADS + g + 1) * HEAD_DIM], pad], axis=0).astype(BF16)
            vn = jnp.concatenate([new_ref[:, (3 * KV_HEADS + g) * HEAD_DIM:(3 * KV_HEADS + g + 1) * HEAD_DIM], pad], axis=0).astype(BF16)
            s = lax.dot_general(qg, kn, _NT, preferred_element_type=F32)
            picked = sel_ref[g, n_steps][:, 0:2 * SUBLANE].astype(F32) > 0.5
            s = jnp.where(new_ok & picked, s, NEG)
            m, l, acc = _softmax_update(s, vn, m_ref[g], l_ref[g], acc_ref[g])
            o_sel = acc / l
            kw = cwin_ref[pl.ds(g, WINDOW, stride=WIN_SLABS), :].astype(BF16)
            vw = cwin_ref[pl.ds(KV_HEADS + g, WINDOW, stride=WIN_SLABS), :].astype(BF16)
            s = lax.dot_general(qg, kw, _NT, preferred_element_type=F32)
            s = jnp.where(wpos > qi, s, NEG)
            m, l, acc = _softmax_update(s, vw, jnp.full((rows, 1), NEG, F32), jnp.zeros((rows, 1), F32),
                                        jnp.zeros((rows, HEAD_DIM), F32))
            kn = jnp.concatenate([nwin_ref[:, g * HEAD_DIM:(g + 1) * HEAD_DIM], pad], axis=0).astype(BF16)
            vn = jnp.concatenate([nwin_ref[:, (KV_HEADS + g) * HEAD_DIM:(KV_HEADS + g + 1) * HEAD_DIM], pad], axis=0).astype(BF16)
            s = lax.dot_general(qg, kn, _NT, preferred_element_type=F32)
            s = jnp.where(new_ok, s, NEG)
            m, l, acc = _softmax_update(s, vn, m, l, acc)
            o_win = acc / l
            o_cmp = ocmp_ref[g]
            for r in range(Q_PER_KV):
                rs = slice(r * ds, (r + 1) * ds)
                c0 = g * LANE
                g_cmp = gates_ref[:, c0 + r:c0 + r + 1]
                g_sel = gates_ref[:, c0 + Q_PER_KV + r:c0 + Q_PER_KV + r + 1]
                g_win = gates_ref[:, c0 + 2 * Q_PER_KV + r:c0 + 2 * Q_PER_KV + r + 1]
                h = g * Q_PER_KV + r
                o_ref[:, h * HEAD_DIM:(h + 1) * HEAD_DIM] = g_cmp * o_cmp[rs] + g_sel * o_sel[rs] + g_win * o_win[rs]
        wout_ref[0:(WINDOW - ds) * WIN_SLABS, :] = cwin_ref[ds * WIN_SLABS:WINDOW * WIN_SLABS, :]
        wout_ref[(WINDOW - ds) * WIN_SLABS:WINDOW * WIN_SLABS, :] = nwin4_ref[...]


def _sample_attention(page_table, cache, layer, q, gates, kvc, kv4, kwin, kwin4, cache_win, expand, db, ds, n_pages):
    n_steps = n_pages // PAGES_PER_STEP
    past_len = n_pages * PAGE_SIZE
    assert ds == SUBLANE and past_len % BLOCK == 0 and past_len >= WINDOW and cache_win.shape[2] == WINDOW * WIN_SLABS
    rows = Q_PER_KV * ds
    per_b = lambda c: pl.BlockSpec((ds, c), lambda b, s, pt: (b, 0))
    return pl.pallas_call(
        functools.partial(_sample_attn_kernel, ds=ds, past_len=past_len, n_steps=n_steps),
        grid_spec=pltpu.PrefetchScalarGridSpec(
            num_scalar_prefetch=1, grid=(db, n_steps),
            in_specs=_page_specs(layer, n_pages) + [
                per_b(D_ATTN), per_b(KV_HEADS * LANE),
                pl.BlockSpec((None, 4, past_len // BLOCK, HEAD_DIM), lambda b, s, pt: (b, 0, 0, 0)),
                per_b(2 * HALF), per_b(HALF),
                pl.BlockSpec((ds * WIN_SLABS, HEAD_DIM), lambda b, s, pt: (b, 0)),
                pl.BlockSpec((None, None, WINDOW * WIN_SLABS, HEAD_DIM), lambda b, s, pt: (layer, b, 0, 0)),
                pl.BlockSpec(expand.shape, lambda b, s, pt: (0, 0))],
            out_specs=[per_b(D_ATTN),
                       pl.BlockSpec((None, WINDOW * WIN_SLABS, HEAD_DIM), lambda b, s, pt: (b, 0, 0))],
            scratch_shapes=[pltpu.VMEM((KV_HEADS, ROWS_PER_STEP, HEAD_DIM), BF16),
                            pltpu.VMEM((KV_HEADS, ROWS_PER_STEP, HEAD_DIM), BF16),
                            pltpu.VMEM((KV_HEADS, n_steps + 1, rows, LANE), BF16),
                            pltpu.VMEM((KV_HEADS, rows, HEAD_DIM), F32),
                            pltpu.VMEM((KV_HEADS, rows, 1), F32),
                            pltpu.VMEM((KV_HEADS, rows, 1), F32),
                            pltpu.VMEM((KV_HEADS, rows, HEAD_DIM), F32)]),
        out_shape=[jax.ShapeDtypeStruct((db * ds, D_ATTN), F32),
                   jax.ShapeDtypeStruct((db, WINDOW * WIN_SLABS, HEAD_DIM), F32)],
        compiler_params=_params(2),
        name="sample_attention",
    )(page_table, *([cache] * PAGES_PER_STEP), q, gates, kvc, kv4, kwin, kwin4, cache_win, expand)


def _gate_columns(w_gates):
    d = w_gates.shape[0]
    w = w_gates.reshape(d, 3, KV_HEADS, Q_PER_KV).transpose(0, 2, 1, 3).reshape(d, KV_HEADS, 3 * Q_PER_KV)
    return jnp.pad(w, ((0, 0), (0, 0), (0, LANE - 3 * Q_PER_KV))).reshape(d, KV_HEADS * LANE)


def kernel(x_prompt, x_sample, cache_kv, cache_win, state_conv, state_h, page_table, w_in, cmp_pe, cmp_w1, cmp_w2, conv_w, conv_b, lru_w_a, lru_b_a, lru_w_i, lru_b_i, lru_lam, group_norm, w_out, ln_g, ln_b, router_w, router_b, exp_w_gate, exp_w_up, exp_w_down):
    b, t, _ = x_prompt.shape
    db, ds, _ = x_sample.shape
    past_len = page_table.shape[1] * PAGE_SIZE
    n_p, n_s = b * t, db * ds
    tm_p, tm_moe, tq, tk, tt = 256, 512, 256, 1024, 256
    n_s_pad = -(-n_s // MOE_CHUNK) * MOE_CHUNK

    xp = x_prompt.reshape(n_p, D_MODEL)
    xs = x_sample.reshape(n_s, D_MODEL)
    expand = (jnp.arange(t)[None, :] // BLOCK == jnp.arange(t // BLOCK)[:, None]).astype(BF16)
    ltri = lambda m: (jnp.arange(m)[None, :] < jnp.arange(m)[:, None]).astype(BF16)
    ltri_p, ltri_s = ltri(tm_moe), ltri(n_s_pad)
    rw = jnp.pad(router_w, ((0, 0), (0, LANE - N_EXPERTS))).astype(BF16)
    rb = jnp.pad(router_b, (0, LANE - N_EXPERTS)).reshape(1, LANE)
    conv0_p = jnp.zeros((b, SUBLANE, D_RNN), F32)
    h0_p = jnp.zeros((b, SUBLANE, D_RNN), F32)
    pad_rows = SUBLANE - (CONV_WIDTH - 1)
    n_pages = page_table.shape[1]
    pt_flat = page_table.reshape(-1)
    cache = cache_kv.reshape(DEPTH, cache_kv.shape[1], PAGE_SIZE * ROW_SLABS, HEAD_DIM)
    cwin = cache_win.reshape(DEPTH, db, cache_win.shape[2] * WIN_SLABS, HEAD_DIM)
    expand_s = (jnp.arange(ROWS_PER_STEP)[None, :] // BLOCK == jnp.arange(LANE)[:, None]).astype(BF16)

    outs = [[] for _ in range(8)]
    for l in range(DEPTH):
        c_gate = 1024 + 6 * KV_HEADS * HEAD_DIM
        w_main = jnp.concatenate([w_in[l, :, :c_gate], w_in[l, :, c_gate + 3 * N_HEADS:]], axis=1).astype(BF16)
        w_gate = _gate_columns(w_in[l, :, c_gate:c_gate + 3 * N_HEADS]).astype(BF16)
        w1 = cmp_w1[l].astype(BF16)
        w2 = cmp_w2[l].astype(BF16)
        w_ai = jnp.concatenate([lru_w_a[l], lru_w_i[l]], axis=-1).astype(BF16)
        vec = lambda v: v.reshape(1, -1)
        lru = (conv_w[l], vec(conv_b[l]), w_ai, vec(lru_b_a[l]), vec(lru_b_i[l]), vec(lru_lam[l]))
        wo = w_out[l].astype(BF16)
        wg, wu, wd = exp_w_gate[l].astype(BF16), exp_w_up[l].astype(BF16), exp_w_down[l].astype(BF16)
        gn = vec(group_norm[l])

        q, kv4, kwin, kvb, gates, u, gg = _in_proj(xp, w_main, w_gate, tm_p)
        kvc = _compress(kv4, cmp_pe[l], w1, w2, b, t)
        attn = _prompt_attention(q, gates, kvc, kvb, expand, b, t, tq, tk)
        rnn, hl = _rglru(u, gg, conv0_p, h0_p, *lru, b, t, tt)
        x1, x1b, rt = _mixer(attn, rnn, xp, gn, wo, vec(ln_g[l, 0]), vec(ln_b[l, 0]), rw, rb, tm_p)
        y = _moe(x1b, rt, ltri_p, wg, wu, wd, tm_moe)
        xp = _ln2(x1, y, vec(ln_g[l, 1]), vec(ln_b[l, 1]), tm_p)
        outs[0].append(kv4.reshape(b, t, KV_SLOTS, KV_HEADS, HEAD_DIM))
        outs[2].append(kwin.reshape(b, t, 2, KV_HEADS, HEAD_DIM)[:, -min(WINDOW, t):])
        outs[4].append(u.reshape(b, t, D_RNN)[:, -(CONV_WIDTH - 1):])
        outs[6].append(hl.reshape(b, D_RNN))

        q, kv4, kwin, kvb, gates, u, gg = _in_proj(xs, w_main, w_gate, n_s)
        kvc = _sample_compress(pt_flat, cache, l, cmp_pe[l], w1.reshape(2, BLOCK // 2, 2 * HEAD_DIM, HEAD_DIM),
                               w2, db, n_pages)
        attn, wnew = _sample_attention(pt_flat, cache, l, q.astype(F32), gates, kvc, kv4, kwin,
                                       kwin.reshape(n_s * WIN_SLABS, HEAD_DIM), cwin, expand_s, db, ds, n_pages)
        conv0_s = jnp.pad(state_conv[l], ((0, 0), (pad_rows, 0), (0, 0)))
        h0_s = jnp.pad(state_h[l][:, None, :], ((0, 0), (0, SUBLANE - 1), (0, 0)))
        rnn, hl = _rglru(u, gg, conv0_s, h0_s, *lru, db, ds, ds)
        x1, x1b, rt = _mixer(attn, rnn, xs, gn, wo, vec(ln_g[l, 0]), vec(ln_b[l, 0]),
                        rw, rb, n_s)
        x1p = jnp.pad(x1b, ((0, n_s_pad - n_s), (0, 0)))
        rtp = jnp.pad(rt, ((0, n_s_pad - n_s), (0, 0)), constant_values=-1.0)
        y = _moe_single_tile(x1p, rtp, ltri_s, wg, wu, wd)[:n_s]
        xs = _ln2(x1, y, vec(ln_g[l, 1]), vec(ln_b[l, 1]), n_s)
        outs[1].append(kv4.reshape(db, ds, KV_SLOTS, KV_HEADS, HEAD_DIM))
        outs[3].append(wnew.reshape(db, WINDOW, 2, KV_HEADS, HEAD_DIM))
        xpc = jnp.concatenate([state_conv[l], u.reshape(db, ds, D_RNN)], axis=1)
        outs[5].append(xpc[:, -(CONV_WIDTH - 1):])
        outs[7].append(hl.reshape(db, D_RNN))

    st = [jnp.stack(o) for o in outs]
    return (xp.reshape(b, t, D_MODEL), xs.reshape(db, ds, D_MODEL),
            st[0], st[1], st[2], st[3], st[4], st[5], st[6], st[7])
```

```python
import functools

import jax
import jax.numpy as jnp
from jax import lax
from jax.experimental import pallas as pl
from jax.experimental.pallas import tpu as pltpu

F32 = jnp.float32
BF16 = jnp.bfloat16
I32 = jnp.int32

D_MODEL = 2048
DEPTH = 4
PAGE_SIZE = 128
D_ATTN = D_MODEL // 2
D_RNN = D_MODEL - D_ATTN
N_HEADS = 8
HEAD_DIM = D_ATTN // N_HEADS
KV_HEADS = 2
Q_PER_KV = N_HEADS // KV_HEADS
BLOCK = 64
N_SELECT = 16
WINDOW = 512
FORCED_SCORE = 1e4
KV_SLOTS = 4
CONV_WIDTH = 4
LRU_BLOCKS = 8
LRU_BW = D_RNN // LRU_BLOCKS
LRU_C = 8.0
N_EXPERTS = 16
N_GROUPS = 4
EXPERTS_PER_GROUP = N_EXPERTS // N_GROUPS
D_EXPERT = 512
ALPHA = (2 * DEPTH) ** 0.25
LN_EPS = 1e-5
SCALE = HEAD_DIM ** -0.5
Q_SCALE = SCALE * 1.4426950408889634
NEG = -1e30

LANE = 128
SUBLANE = 8
MOE_CHUNK = 152
MOE_CHUNK_PAD = 256
VMEM_LIMIT = 56 * 1024 * 1024

_NT = (((1,), (1,)), ((), ()))
_TN = (((0,), (0,)), ((), ()))


def _params(n_axes):
    return pltpu.CompilerParams(dimension_semantics=("arbitrary",) * n_axes,
                                vmem_limit_bytes=VMEM_LIMIT)


def _sigmoid(x):
    return 1.0 / (1.0 + jnp.exp(-x))


def _gelu(x):
    return 0.5 * x * (1.0 + jnp.tanh(0.7978845608028654 * (x + 0.044715 * (x * x * x))))


def _dot(a, b):
    return jnp.dot(a, b, preferred_element_type=F32)


def _resident(shape):
    nd = len(shape)
    return pl.BlockSpec(shape, lambda *_: (0,) * nd, pipeline_mode=pl.Buffered(1))


def _in_proj_kernel(x_ref, w_ref, wg_ref, q_ref, kv4_ref, kwin_ref, kvb_ref, gates_ref, u_ref, gg_ref):
    xb = x_ref[...].astype(BF16)

    def mm(lo, hi):
        return _dot(xb, w_ref[:, lo:hi])

    q_ref[...] = (mm(0, 1024) * Q_SCALE).astype(BF16)
    kv = mm(1024, 2048)
    kv4_ref[...] = kv
    kw = mm(2048, 2560)
    kwin_ref[...] = kw
    kvb_ref[:, 0:512] = kv[:, 512:1024].astype(BF16)
    kvb_ref[:, 512:1024] = kw.astype(BF16)
    gates_ref[...] = _sigmoid(_dot(xb, wg_ref[...]))
    u_ref[...] = mm(2560, 3584)
    gg_ref[...] = mm(3584, 4608)


def _in_proj(x, w_main, w_gate, tm):
    n = x.shape[0]
    row = lambda c: pl.BlockSpec((tm, c), lambda i: (i, 0))
    return pl.pallas_call(
        _in_proj_kernel,
        grid=(n // tm,),
        in_specs=[row(D_MODEL), _resident(w_main.shape), _resident(w_gate.shape)],
        out_specs=[row(1024), row(1024), row(512), row(1024), row(256), row(1024), row(1024)],
        out_shape=[jax.ShapeDtypeStruct((n, 1024), BF16),
                   jax.ShapeDtypeStruct((n, 1024), F32),
                   jax.ShapeDtypeStruct((n, 512), F32),
                   jax.ShapeDtypeStruct((n, 1024), BF16),
                   jax.ShapeDtypeStruct((n, 256), F32),
                   jax.ShapeDtypeStruct((n, 1024), F32),
                   jax.ShapeDtypeStruct((n, 1024), F32)],
        compiler_params=_params(1),
        name="in_proj",
    )(x, w_main, w_gate)


def _compress_kernel(x0_ref, x1_ref, x2_ref, x3_ref, pe_ref, w1_ref, w2_ref, o_ref, *, nblk):
    x_refs = (x0_ref, x1_ref, x2_ref, x3_ref)
    for s in range(2):
        acc = [jnp.zeros((nblk, HEAD_DIM), F32) for _ in range(KV_HEADS)]
        for l in range(BLOCK):
            pe = pe_ref[s, l:l + 1, :]
            w = w1_ref[s, l]
            for g in range(KV_HEADS):
                rows = x_refs[s * KV_HEADS + g][pl.ds(l, nblk, stride=BLOCK), :]
                acc[g] = acc[g] + _dot((rows + pe).astype(BF16), w)
        for g in range(KV_HEADS):
            h = _gelu(acc[g]).astype(BF16)
            o_ref[s * KV_HEADS + g] = _dot(h, w2_ref[s]).astype(BF16)


def _compress(kv4, pe, w1, w2, batch, t_len):
    nblk = t_len // BLOCK
    col = lambda c: pl.BlockSpec((t_len, HEAD_DIM), lambda b: (b, c))
    return pl.pallas_call(
        functools.partial(_compress_kernel, nblk=nblk),
        grid=(batch,),
        in_specs=[col(0), col(1), col(2), col(3),
                  _resident(pe.shape), _resident(w1.shape), _resident(w2.shape)],
        out_specs=pl.BlockSpec((None, 4, nblk, HEAD_DIM), lambda b: (b, 0, 0, 0)),
        out_shape=jax.ShapeDtypeStruct((batch, 4, nblk, HEAD_DIM), BF16),
        compiler_params=_params(1),
        name="compress",
    )(kv4, kv4, kv4, kv4, pe, w1, w2)


def _attn_kernel(q_ref, gates_ref, kc_ref, vc_ref, ks_ref, vs_ref, kw_ref, vw_ref, e_ref, o_ref,
                 *, tq, tk, t_len):
    i = pl.program_id(2)
    s0 = i * tq
    nblk = t_len // BLOCK
    qpos = s0 + lax.broadcasted_iota(I32, (tq, 1), 0)
    qh = [q_ref[:, r * HEAD_DIM:(r + 1) * HEAD_DIM] for r in range(Q_PER_KV)]

    blk = lax.broadcasted_iota(I32, (nblk, tq), 0)
    qpos_l = s0 + lax.broadcasted_iota(I32, (1, tq), 1)
    cmask = (blk + 1) * BLOCK - 1 <= qpos_l
    kc = kc_ref[...]
    vc = vc_ref[...]
    imp = jnp.zeros((nblk, tq), F32)
    o_cmp = []
    for r in range(Q_PER_KV):
        sc = jnp.where(cmask, lax.dot_general(kc, qh[r], _NT, preferred_element_type=F32), NEG)
        m = jnp.max(sc, axis=0, keepdims=True)
        e = jnp.where(cmask, jnp.exp2(sc - m), 0.0)
        p = e / jnp.maximum(jnp.sum(e, axis=0, keepdims=True), 1e-30)
        imp = imp + p
        o_cmp.append(lax.dot_general(p.astype(BF16), vc, _TN, preferred_element_type=F32))

    cur = jnp.right_shift(qpos_l, 6)
    forced = (blk == 0) | (blk == cur) | (blk == cur - 1)
    valid = blk <= cur
    score = jnp.where(valid, jnp.where(forced, FORCED_SCORE, imp), -1.0)
    rank = jnp.zeros((nblk, tq), I32)
    for j in range(nblk):
        cj = score[j:j + 1, :]
        beats = (cj > score) | ((cj == score) & (blk > j))
        rank = rank + beats.astype(I32)
    sel = ((rank < min(N_SELECT, nblk)) & valid).astype(BF16)

    def sel_body(kt, carry):
        off = pl.multiple_of(kt * tk, tk)
        k = ks_ref[pl.ds(off, tk), :]
        v = vs_ref[pl.ds(off, tk), :]
        hit = lax.dot_general(sel, e_ref[:, pl.ds(off, tk)], _TN, preferred_element_type=F32)
        kpos = off + lax.broadcasted_iota(I32, (tq, tk), 1)
        bias = jnp.where((hit > 0.5) & (kpos <= qpos), 0.0, NEG)
        out = []
        for r in range(Q_PER_KV):
            m, l, acc = carry[r]
            s = lax.dot_general(qh[r], k, _NT, preferred_element_type=F32) + bias
            m_new = jnp.maximum(m, jnp.max(s, axis=1, keepdims=True))
            a = jnp.exp2(m - m_new)
            p = jnp.exp2(s - m_new)
            l = a * l + jnp.sum(p, axis=1, keepdims=True)
            acc = a * acc + _dot(p.astype(BF16), v)
            out.append((m_new, l, acc))
        return tuple(out)

    init = tuple((jnp.full((tq, 1), NEG, F32), jnp.zeros((tq, 1), F32), jnp.zeros((tq, HEAD_DIM), F32))
                 for _ in range(Q_PER_KV))
    n_t = (s0 + tq + tk - 1) // tk
    fin = lax.fori_loop(0, n_t, sel_body, init)
    o_sel = [acc / l for (_, l, acc) in fin]

    wlen = WINDOW + tq
    start = pl.multiple_of(jnp.maximum(s0 - WINDOW, 0), tq)
    kw = kw_ref[pl.ds(start, wlen), :]
    vw = vw_ref[pl.ds(start, wlen), :]
    kposw = start + lax.broadcasted_iota(I32, (tq, wlen), 1)
    biasw = jnp.where((kposw <= qpos) & (kposw > qpos - WINDOW), 0.0, NEG)

    for r in range(Q_PER_KV):
        s = lax.dot_general(qh[r], kw, _NT, preferred_element_type=F32) + biasw
        p = jnp.exp2(s - jnp.max(s, axis=1, keepdims=True))
        o_win = _dot(p.astype(BF16), vw) / jnp.sum(p, axis=1, keepdims=True)
        g_cmp = gates_ref[:, r:r + 1]
        g_sel = gates_ref[:, Q_PER_KV + r:Q_PER_KV + r + 1]
        g_win = gates_ref[:, 2 * Q_PER_KV + r:2 * Q_PER_KV + r + 1]
        o_ref[:, r * HEAD_DIM:(r + 1) * HEAD_DIM] = g_cmp * o_cmp[r] + g_sel * o_sel[r] + g_win * o_win


def _prompt_attention(q, gates, kvc, kvb, expand, batch, t_len, tq, tk):
    nq = t_len // tq
    gw = Q_PER_KV * HEAD_DIM
    col = lambda c: pl.BlockSpec((t_len, HEAD_DIM), lambda b, g, i: (b, c + g))
    cmp_blk = lambda c: pl.BlockSpec((None, None, t_len // BLOCK, HEAD_DIM), lambda b, g, i: (b, c + g, 0, 0))
    return pl.pallas_call(
        functools.partial(_attn_kernel, tq=tq, tk=tk, t_len=t_len),
        grid=(batch, KV_HEADS, nq),
        in_specs=[pl.BlockSpec((tq, gw), lambda b, g, i: (b * nq + i, g)),
                  pl.BlockSpec((tq, LANE), lambda b, g, i: (b * nq + i, g)),
                  cmp_blk(0), cmp_blk(2),
                  col(0), col(2), col(4), col(6),
                  pl.BlockSpec(expand.shape, lambda b, g, i: (0, 0))],
        out_specs=pl.BlockSpec((tq, gw), lambda b, g, i: (b * nq + i, g)),
        out_shape=jax.ShapeDtypeStruct((batch * t_len, D_ATTN), F32),
        compiler_params=_params(3),
        name="prompt_attention",
    )(q, gates, kvc, kvc, kvb, kvb, kvb, kvb, expand)


def _rglru_kernel(u_ref, gg_ref, c0_ref, h0_ref, cw_ref, cb_ref, wai_ref, ba_ref, bi_ref, lam_ref,
                  y_ref, hl_ref, ext_ref, a_ref, b_ref, tail_ref, h_ref, *, tt):
    t = pl.program_id(1)

    @pl.when(t == 0)
    def _():
        tail_ref[...] = c0_ref[...]
        h_ref[...] = h0_ref[...]

    u = u_ref[...]
    ext_ref[0:SUBLANE, :] = tail_ref[...]
    ext_ref[SUBLANE:SUBLANE + tt, :] = u
    base = SUBLANE - (CONV_WIDTH - 1)
    xc = cb_ref[...] + ext_ref[base:base + tt, :] * cw_ref[0:1, :]
    for j in range(1, CONV_WIDTH - 1):
        xc = xc + ext_ref[base + j:base + j + tt, :] * cw_ref[j:j + 1, :]
    xc = xc + u * cw_ref[CONV_WIDTH - 1:CONV_WIDTH, :]
    tail_ref[...] = ext_ref[tt:tt + SUBLANE, :]

    xcb = xc.astype(BF16)
    for n in range(LRU_BLOCKS):
        c = slice(n * LRU_BW, (n + 1) * LRU_BW)
        ai = _dot(xcb[:, c], wai_ref[n])
        r = _sigmoid(ai[:, :LRU_BW] + ba_ref[:, c])
        ig = _sigmoid(ai[:, LRU_BW:] + bi_ref[:, c])
        nl = -lam_ref[:, c]
        softplus = jnp.maximum(nl, 0.0) + jnp.log1p(jnp.exp(-jnp.abs(nl)))
        log_a = (-LRU_C * softplus) * r
        a = jnp.exp(log_a)
        gain = jnp.sqrt(-jnp.tanh(log_a) * (a * a + 1.0))
        a_ref[:, c] = a
        b_ref[:, c] = gain * (ig * xc[:, c])

    def group(gi, h):
        off = pl.multiple_of(gi * SUBLANE, SUBLANE)
        a8 = a_ref[pl.ds(off, SUBLANE), :]
        b8 = b_ref[pl.ds(off, SUBLANE), :]
        rows = []
        for r in range(SUBLANE):
            h = a8[r:r + 1, :] * h + b8[r:r + 1, :]
            rows.append(h)
        a_ref[pl.ds(off, SUBLANE), :] = jnp.concatenate(rows, axis=0)
        return h

    h = lax.fori_loop(0, tt // SUBLANE, group, h_ref[0:1, :])
    h_ref[0:1, :] = h
    hl_ref[...] = h
    y_ref[...] = a_ref[...] * _gelu(gg_ref[...])


def _rglru(u, gg, conv0, h0, conv_w, conv_b, w_ai, b_a, b_i, lam, batch, t_len, tt):
    nt = t_len // tt
    c = D_RNN
    tile = pl.BlockSpec((tt, c), lambda b, t: (b * nt + t, 0))
    vec = _resident((1, c))
    return pl.pallas_call(
        functools.partial(_rglru_kernel, tt=tt),
        grid=(batch, nt),
        in_specs=[tile, tile,
                  pl.BlockSpec((None, SUBLANE, c), lambda b, t: (b, 0, 0)),
                  pl.BlockSpec((None, SUBLANE, c), lambda b, t: (b, 0, 0)),
                  _resident(conv_w.shape), vec, _resident(w_ai.shape), vec, vec, vec],
        out_specs=[tile, pl.BlockSpec((None, 1, c), lambda b, t: (b, 0, 0))],
        out_shape=[jax.ShapeDtypeStruct((batch * t_len, c), F32),
                   jax.ShapeDtypeStruct((batch, 1, c), F32)],
        scratch_shapes=[pltpu.VMEM((tt + SUBLANE, c), F32), pltpu.VMEM((tt, c), F32),
                        pltpu.VMEM((tt, c), F32), pltpu.VMEM((SUBLANE, c), F32),
                        pltpu.VMEM((SUBLANE, c), F32)],
        compiler_params=_params(2),
        name="rglru",
    )(u, gg, conv0, h0, conv_w, conv_b, w_ai, b_a, b_i, lam)


def _layer_norm(z, g, b):
    zc = z - jnp.mean(z, axis=1, keepdims=True)
    var = jnp.mean(zc * zc, axis=1, keepdims=True)
    return zc * lax.rsqrt(var + LN_EPS) * g + b


def _mixer_kernel(attn_ref, rnn_ref, x_ref, gn_ref, wo_ref, lng_ref, lnb_ref, rw_ref, rb_ref,
                  x1_ref, x1b_ref, rt_ref):
    def rms(v, g):
        return (v * lax.rsqrt(jnp.mean(v * v, axis=1, keepdims=True) + LN_EPS) * g).astype(BF16)

    an = rms(attn_ref[...], gn_ref[:, 0:D_ATTN])
    rn = rms(rnn_ref[...], gn_ref[:, D_ATTN:D_MODEL])
    y = _dot(an, wo_ref[0:D_ATTN, :]) + _dot(rn, wo_ref[D_ATTN:D_MODEL, :])
    x1 = _layer_norm(ALPHA * x_ref[...] + y, lng_ref[...], lnb_ref[...])
    x1_ref[...] = x1
    x1b = x1.astype(BF16)
    x1b_ref[...] = x1b

    s = _sigmoid(_dot(x1b, rw_ref[...]))
    biased = s + rb_ref[...]
    sc = [s[:, e:e + 1] for e in range(N_EXPERTS)]
    bc = [biased[:, e:e + 1] for e in range(N_EXPERTS)]
    best = grp = None
    for g in range(N_GROUPS):
        v = bc[g * EXPERTS_PER_GROUP:(g + 1) * EXPERTS_PER_GROUP]
        top2 = None
        for i in range(EXPERTS_PER_GROUP):
            for j in range(i + 1, EXPERTS_PER_GROUP):
                pair = v[i] + v[j]
                top2 = pair if top2 is None else jnp.maximum(top2, pair)
        if g == 0:
            best, grp = top2, jnp.zeros_like(top2, dtype=I32)
        else:
            better = top2 > best
            grp = jnp.where(better, g, grp)
            best = jnp.where(better, top2, best)

    def in_group(cols, j):
        out = cols[j]
        for g in range(1, N_GROUPS):
            out = jnp.where(grp == g, cols[g * EXPERTS_PER_GROUP + j], out)
        return out

    bv = [in_group(bc, j) for j in range(EXPERTS_PER_GROUP)]
    sv = [in_group(sc, j) for j in range(EXPERTS_PER_GROUP)]

    def argmax_first(vals, skip=None):
        bestv = idx = None
        for j, v in enumerate(vals):
            vj = v if skip is None else jnp.where(skip == j, -jnp.inf, v)
            if j == 0:
                bestv, idx = vj, jnp.zeros_like(grp)
            else:
                better = vj > bestv
                idx = jnp.where(better, j, idx)
                bestv = jnp.where(better, vj, bestv)
        return idx

    i1 = argmax_first(bv)
    i2 = argmax_first(bv, skip=i1)

    def pick(vals, idx):
        out = vals[0]
        for j in range(1, len(vals)):
            out = jnp.where(idx == j, vals[j], out)
        return out

    s1 = pick(sv, i1)
    s2 = pick(sv, i2)
    den = s1 + s2
    w1 = s1 / den
    w2 = s2 / den
    lane = lax.broadcasted_iota(I32, rt_ref.shape, 1)
    rt = jnp.where(lane == EXPERTS_PER_GROUP, grp.astype(F32), 0.0)
    rt = rt + jnp.where(lane == i1, w1, 0.0) + jnp.where(lane == i2, w2, 0.0)
    rt_ref[...] = rt


def _mixer(attn, rnn, x, gn, w_out, ln_g, ln_b, rw, rb, tm):
    n = x.shape[0]
    row = lambda c: pl.BlockSpec((tm, c), lambda i: (i, 0))
    vec = _resident((1, D_MODEL))
    return pl.pallas_call(
        _mixer_kernel,
        grid=(n // tm,),
        in_specs=[row(D_ATTN), row(D_RNN), row(D_MODEL), vec, _resident(w_out.shape), vec, vec,
                  _resident(rw.shape), _resident(rb.shape)],
        out_specs=[row(D_MODEL), row(D_MODEL), row(LANE)],
        out_shape=[jax.ShapeDtypeStruct((n, D_MODEL), F32),
                   jax.ShapeDtypeStruct((n, D_MODEL), BF16),
                   jax.ShapeDtypeStruct((n, LANE), F32)],
        compiler_params=_params(1),
        name="mixer_out",
    )(attn, rnn, x, gn, w_out, ln_g, ln_b, rw, rb)


def _split3(v):
    a = v.astype(BF16)
    r = v - a.astype(F32)
    b = r.astype(BF16)
    c = (r - b.astype(F32)).astype(BF16)
    return a, b, c


def _moe_kernel(*refs, tm, group, accumulate):
    if accumulate:
        x_ref, rt_ref, acc_ref, ltri_ref, wg_ref, wu_ref, wd_ref, o_ref = refs
    else:
        x_ref, rt_ref, ltri_ref, wg_ref, wu_ref, wd_ref, o_ref = refs
    g = pl.program_id(1) if group is None else group
    rt = rt_ref[...]
    member = rt[:, EXPERTS_PER_GROUP:EXPERTS_PER_GROUP + 1] == jnp.asarray(g, F32)
    mf = member.astype(F32)
    pos = _dot(ltri_ref[...], jnp.broadcast_to(mf, (tm, LANE)).astype(BF16))[:, 0:1].astype(I32)
    count = jnp.sum(mf).astype(I32)
    xb = x_ref[...]
    c1, c2, c3 = _split3(rt)
    lane = lax.broadcasted_iota(I32, (tm, MOE_CHUNK_PAD), 1)
    y_pad = jnp.zeros((MOE_CHUNK_PAD - MOE_CHUNK, D_MODEL), BF16)
    if accumulate:
        o_ref[...] = acc_ref[...]
    elif group is None:
        @pl.when(g == 0)
        def _():
            o_ref[...] = jnp.zeros_like(o_ref)
    else:
        o_ref[...] = jnp.zeros_like(o_ref)

    def chunk(k, carry):
        ptf = jnp.where(member & (pos - k * MOE_CHUNK == lane) & (lane < MOE_CHUNK), 1.0, 0.0)
        pt = ptf.astype(BF16)
        p = ptf.T[0:MOE_CHUNK].astype(BF16)
        xc = _dot(p, xb).astype(BF16)
        cw = _dot(p, c1) + _dot(p, c2) + _dot(p, c3)
        y = jnp.zeros((MOE_CHUNK, D_MODEL), F32)
        for j in range(EXPERTS_PER_GROUP):
            hg = _dot(xc, wg_ref[j])
            hu = _dot(xc, wu_ref[j])
            h = (hg * _sigmoid(hg)) * hu * cw[:, j:j + 1]
            y = y + _dot(h.astype(BF16), wd_ref[j])
        yh = y.astype(BF16)
        yl = (y - yh.astype(F32)).astype(BF16)
        o_ref[...] += (_dot(pt, jnp.concatenate([yh, y_pad], axis=0))
                       + _dot(pt, jnp.concatenate([yl, y_pad], axis=0)))
        return carry

    lax.fori_loop(0, (count + MOE_CHUNK - 1) // MOE_CHUNK, chunk, 0)


def _moe_single_tile(x, rt, ltri, wg, wu, wd):
    n = x.shape[0]
    row = lambda c: pl.BlockSpec((n, c), lambda t, g: (0, 0))
    grp = lambda a, b: pl.BlockSpec((EXPERTS_PER_GROUP, a, b), lambda t, g: (g, 0, 0))
    return pl.pallas_call(
        functools.partial(_moe_kernel, tm=n, group=None, accumulate=False),
        grid=(1, N_GROUPS),
        in_specs=[row(D_MODEL), row(LANE), pl.BlockSpec(ltri.shape, lambda t, g: (0, 0)),
                  grp(D_MODEL, D_EXPERT), grp(D_MODEL, D_EXPERT), grp(D_EXPERT, D_MODEL)],
        out_specs=row(D_MODEL),
        out_shape=jax.ShapeDtypeStruct((n, D_MODEL), F32),
        compiler_params=_params(2),
        name="grouped_moe_single_tile",
    )(x, rt, ltri, wg, wu, wd)


def _moe(x, rt, ltri, wg, wu, wd, tm):
    n = x.shape[0]
    row = lambda c: pl.BlockSpec((tm, c), lambda t: (t, 0))
    acc = None
    for g in range(N_GROUPS):
        grp = lambda a, b: pl.BlockSpec((EXPERTS_PER_GROUP, a, b), lambda t, g=g: (g, 0, 0),
                                        pipeline_mode=pl.Buffered(1))
        first = acc is None
        acc = pl.pallas_call(
            functools.partial(_moe_kernel, tm=tm, group=g, accumulate=not first),
            grid=(n // tm,),
            in_specs=[row(D_MODEL), row(LANE)] + ([] if first else [row(D_MODEL)]) + [
                _resident(ltri.shape), grp(D_MODEL, D_EXPERT), grp(D_MODEL, D_EXPERT), grp(D_EXPERT, D_MODEL)],
            out_specs=row(D_MODEL),
            out_shape=jax.ShapeDtypeStruct((n, D_MODEL), F32),
            input_output_aliases={} if first else {2: 0},
            compiler_params=_params(1),
            name="grouped_moe",
        )(*((x, rt) + (() if first else (acc,)) + (ltri, wg, wu, wd)))
    return acc


def _ln2_kernel(x_ref, y_ref, g_ref, b_ref, o_ref):
    o_ref[...] = _layer_norm(ALPHA * x_ref[...] + y_ref[...], g_ref[...], b_ref[...])


def _ln2(x, y, g, b, tm):
    n = x.shape[0]
    row = pl.BlockSpec((tm, D_MODEL), lambda i: (i, 0))
    vec = _resident((1, D_MODEL))
    return pl.pallas_call(
        _ln2_kernel, grid=(n // tm,), in_specs=[row, row, vec, vec], out_specs=row,
        out_shape=jax.ShapeDtypeStruct((n, D_MODEL), F32),
        compiler_params=_params(1), name="moe_residual_ln",
    )(x, y, g, b)


PAGES_PER_STEP = 32
ROWS_PER_STEP = PAGES_PER_STEP * PAGE_SIZE
BLOCKS_PER_STEP = ROWS_PER_STEP // BLOCK
HALF = KV_HEADS * HEAD_DIM * 2
ROW_SLABS = KV_SLOTS * KV_HEADS
WIN_SLABS = 2 * KV_HEADS


def _page_specs(layer, n_pages):
    def spec(j):
        return pl.BlockSpec((None, None, PAGE_SIZE * ROW_SLABS, HEAD_DIM),
                            lambda b, s, pt: (layer, pt[b * n_pages + s * PAGES_PER_STEP + j], 0, 0))
    return [spec(j) for j in range(PAGES_PER_STEP)]


def _slab(page, c):
    return page[pl.ds(c, PAGE_SIZE, stride=ROW_SLABS), :]


def _sample_compress_kernel(pt_ref, *refs):
    pages = refs[:PAGES_PER_STEP]
    pe_ref, w1_ref, w2_ref, o_ref, acc_ref = refs[PAGES_PER_STEP:]
    m_rows = BLOCKS_PER_STEP * ROW_SLABS
    acc = jnp.zeros((m_rows, 2 * HEAD_DIM), F32)
    for lp in range(BLOCK // 2):
        parts = []
        for l in (2 * lp, 2 * lp + 1):
            tiles = [page[(h * BLOCK + l) * ROW_SLABS:(h * BLOCK + l + 1) * ROW_SLABS, :]
                     for page in pages for h in range(PAGE_SIZE // BLOCK)]
            pe = jnp.concatenate([pe_ref[l]] * BLOCKS_PER_STEP, axis=0)
            parts.append(jnp.concatenate(tiles, axis=0) + pe)
        lhs = jnp.concatenate(parts, axis=1).astype(BF16)
        acc = acc + _dot(lhs, w1_ref[lp])
    for s in range(2):
        acc_ref[s] = acc[:, s * HEAD_DIM:(s + 1) * HEAD_DIM]
    for s in range(2):
        for g in range(KV_HEADS):
            c = s * KV_HEADS + g
            h = _gelu(acc_ref[s, pl.ds(c, BLOCKS_PER_STEP, stride=ROW_SLABS), :]).astype(BF16)
            o_ref[c] = _dot(h, w2_ref[s]).astype(BF16)


def _sample_compress(page_table, cache, layer, pe_rows, w1_cat, w2, db, n_pages):
    n_steps = n_pages // PAGES_PER_STEP
    nblk = n_pages * PAGE_SIZE // BLOCK
    const = lambda shape: pl.BlockSpec(shape, lambda b, s, pt: (0,) * len(shape))
    return pl.pallas_call(
        _sample_compress_kernel,
        grid_spec=pltpu.PrefetchScalarGridSpec(
            num_scalar_prefetch=1, grid=(db, n_steps),
            in_specs=_page_specs(layer, n_pages) + [const(pe_rows.shape), const(w1_cat.shape), const(w2.shape)],
            out_specs=pl.BlockSpec((None, 4, BLOCKS_PER_STEP, HEAD_DIM), lambda b, s, pt: (b, 0, s, 0)),
            scratch_shapes=[pltpu.VMEM((2, BLOCKS_PER_STEP * ROW_SLABS, HEAD_DIM), F32)]),
        out_shape=jax.ShapeDtypeStruct((db, 4, nblk, HEAD_DIM), BF16),
        compiler_params=_params(2),
        name="sample_compress",
    )(page_table, *([cache] * PAGES_PER_STEP), pe_rows, w1_cat, w2)


def _compress_operands(pe, w1):
    pe_rows = jnp.zeros((BLOCK, ROW_SLABS, HEAD_DIM), F32)
    for s in range(2):
        for g in range(KV_HEADS):
            pe_rows = pe_rows.at[:, s * KV_HEADS + g, :].set(pe[s])
    w1_cat = w1.transpose(1, 2, 0, 3).reshape(BLOCK // 2, 2 * HEAD_DIM, 2 * HEAD_DIM).astype(BF16)
    return pe_rows, w1_cat


def _softmax_update(s, v, m, l, acc):
    m_new = jnp.maximum(m, jnp.max(s, axis=1, keepdims=True))
    a = jnp.exp2(m - m_new)
    p = jnp.exp2(s - m_new)
    return m_new, a * l + jnp.sum(p, axis=1, keepdims=True), a * acc + _dot(p.astype(BF16), v)


def _sample_attn_kernel(pt_ref, *refs, ds, past_len, n_steps):
    pages = refs[:PAGES_PER_STEP]
    (q_ref, gates_ref, kvc_ref, new_ref, nwin_ref, nwin4_ref, cwin_ref, e_ref,
     o_ref, wout_ref, k_ref, v_ref, sel_ref, ocmp_ref, m_ref, l_ref, acc_ref) = refs[PAGES_PER_STEP:]
    step = pl.program_id(1)
    rows = Q_PER_KV * ds
    nc = past_len // BLOCK
    ns = -(-(past_len + ds) // BLOCK)
    ns_pad = -(-ns // LANE) * LANE
    qi = lax.broadcasted_iota(I32, (rows, 1), 0) % ds
    qpos = past_len + qi

    def q_group(g):
        return jnp.concatenate([q_ref[:, (g * Q_PER_KV + r) * HEAD_DIM:(g * Q_PER_KV + r + 1) * HEAD_DIM]
                                for r in range(Q_PER_KV)], axis=0).astype(BF16)

    @pl.when(step == 0)
    def _():
        blk_c = lax.broadcasted_iota(I32, (rows, nc), 1)
        cmask = (blk_c + 1) * BLOCK - 1 <= qpos
        blk = lax.broadcasted_iota(I32, (ds, ns_pad), 1)
        qpos_q = past_len + lax.broadcasted_iota(I32, (ds, 1), 0)
        cur = qpos_q // BLOCK
        forced = (blk == 0) | (blk == cur) | (blk == cur - 1)
        valid = (blk <= cur) & (blk < ns)
        for g in range(KV_HEADS):
            qg = q_group(g)
            sc = jnp.where(cmask, lax.dot_general(qg, kvc_ref[g], _NT, preferred_element_type=F32), NEG)
            m = jnp.max(sc, axis=1, keepdims=True)
            e = jnp.where(cmask, jnp.exp2(sc - m), 0.0)
            p = e / jnp.maximum(jnp.sum(e, axis=1, keepdims=True), 1e-30)
            ocmp_ref[g] = _dot(p.astype(BF16), kvc_ref[KV_HEADS + g])
            imp = p[0:ds]
            for r in range(1, Q_PER_KV):
                imp = imp + p[r * ds:(r + 1) * ds]
            imp = jnp.concatenate([imp, jnp.zeros((ds, ns_pad - nc), F32)], axis=1)
            score = jnp.where(valid, jnp.where(forced, FORCED_SCORE, imp), -1.0)
            rank = jnp.zeros((ds, ns_pad), I32)
            for j in range(ns):
                cj = score[:, j:j + 1]
                rank = rank + ((cj > score) | ((cj == score) & (blk > j))).astype(I32)
            sel = ((rank < min(N_SELECT, ns)) & valid).astype(F32)
            sel = jnp.concatenate([sel] * Q_PER_KV, axis=0)
            for t in range(n_steps):
                part = sel[:, t * BLOCKS_PER_STEP:(t + 1) * BLOCKS_PER_STEP]
                sel_ref[g, t] = jnp.concatenate(
                    [part, jnp.zeros((rows, LANE - BLOCKS_PER_STEP), F32)], axis=1).astype(BF16)
            sel_ref[g, n_steps] = jnp.broadcast_to(sel[:, nc:nc + 1], (rows, LANE)).astype(BF16)
            m_ref[g] = jnp.full((rows, 1), NEG, F32)
            l_ref[g] = jnp.zeros((rows, 1), F32)
            acc_ref[g] = jnp.zeros((rows, HEAD_DIM), F32)

    for j, page in enumerate(pages):
        for g in range(KV_HEADS):
            k_ref[g, j * PAGE_SIZE:(j + 1) * PAGE_SIZE, :] = _slab(page, 2 * KV_HEADS + g).astype(BF16)
            v_ref[g, j * PAGE_SIZE:(j + 1) * PAGE_SIZE, :] = _slab(page, 3 * KV_HEADS + g).astype(BF16)
    for g in range(KV_HEADS):
        qg = q_group(g)
        hit = _dot(sel_ref[g, step], e_ref[...])
        s = lax.dot_general(qg, k_ref[g], _NT, preferred_element_type=F32)
        s = jnp.where(hit > 0.5, s, NEG)
        m, l, acc = _softmax_update(s, v_ref[g], m_ref[g], l_ref[g], acc_ref[g])
        m_ref[g] = m
        l_ref[g] = l
        acc_ref[g] = acc

    @pl.when(step == n_steps - 1)
    def _():
        pad = jnp.zeros((2 * SUBLANE - ds, HEAD_DIM), F32)
        col = lax.broadcasted_iota(I32, (rows, 2 * SUBLANE), 1)
        new_ok = (col <= qi) & (col < ds)
        wpos = lax.broadcasted_iota(I32, (rows, WINDOW), 1)
        for g in range(KV_HEADS):
            qg = q_group(g)
            kn = jnp.concatenate([new_ref[:, (2 * KV_HEADS + g) * HEAD_DIM:(2 * KV_HEADS + g + 1) * HEAD_DIM], pad], axis=0).astype(BF16)
            vn = jnp.concatenate([new_ref[:, (3 * KV_HEADS + g) * HEAD_DIM:(3 * KV_HEADS + g + 1) * HEAD_DIM], pad], axis=0).astype(BF16)
            s = lax.dot_general(qg, kn, _NT, preferred_element_type=F32)
            picked = sel_ref[g, n_steps][:, 0:2 * SUBLANE].astype(F32) > 0.5
            s = jnp.where(new_ok & picked, s, NEG)
            m, l, acc = _softmax_update(s, vn, m_ref[g], l_ref[g], acc_ref[g])
            o_sel = acc / l
            kw = cwin_ref[pl.ds(g, WINDOW, stride=WIN_SLABS), :].astype(BF16)
            vw = cwin_ref[pl.ds(KV_HEADS + g, WINDOW, stride=WIN_SLABS), :].astype(BF16)
            s = lax.dot_general(qg, kw, _NT, preferred_element_type=F32)
            s = jnp.where(wpos > qi, s, NEG)
            m, l, acc = _softmax_update(s, vw, jnp.full((rows, 1), NEG, F32), jnp.zeros((rows, 1), F32),
                                        jnp.zeros((rows, HEAD_DIM), F32))
            kn = jnp.concatenate([nwin_ref[:, g * HEAD_DIM:(g + 1) * HEAD_DIM], pad], axis=0).astype(BF16)
            vn = jnp.concatenate([nwin_ref[:, (KV_HEADS + g) * HEAD_DIM:(KV_HEADS + g + 1) * HEAD_DIM], pad], axis=0).astype(BF16)
            s = lax.dot_general(qg, kn, _NT, preferred_element_type=F32)
            s = jnp.where(new_ok, s, NEG)
            m, l, acc = _softmax_update(s, vn, m, l, acc)
            o_win = acc / l
            o_cmp = ocmp_ref[g]
            for r in range(Q_PER_KV):
                rs = slice(r * ds, (r + 1) * ds)
                c0 = g * LANE
                g_cmp = gates_ref[:, c0 + r:c0 + r + 1]
                g_sel = gates_ref[:, c0 + Q_PER_KV + r:c0 + Q_PER_KV + r + 1]
                g_win = gates_ref[:, c0 + 2 * Q_PER_KV + r:c0 + 2 * Q_PER_KV + r + 1]
                h = g * Q_PER_KV + r
                o_ref[:, h * HEAD_DIM:(h + 1) * HEAD_DIM] = g_cmp * o_cmp[rs] + g_sel * o_sel[rs] + g_win * o_win[rs]
        wout_ref[0:(WINDOW - ds) * WIN_SLABS, :] = cwin_ref[ds * WIN_SLABS:WINDOW * WIN_SLABS, :]
        wout_ref[(WINDOW - ds) * WIN_SLABS:WINDOW * WIN_SLABS, :] = nwin4_ref[...]


def _sample_attention(page_table, cache, layer, q, gates, kvc, kv4, kwin, kwin4, cache_win, expand, db, ds, n_pages):
    n_steps = n_pages // PAGES_PER_STEP
    past_len = n_pages * PAGE_SIZE
    assert ds == SUBLANE and past_len % BLOCK == 0 and past_len >= WINDOW and cache_win.shape[2] == WINDOW * WIN_SLABS
    rows = Q_PER_KV * ds
    per_b = lambda c: pl.BlockSpec((ds, c), lambda b, s, pt: (b, 0))
    return pl.pallas_call(
        functools.partial(_sample_attn_kernel, ds=ds, past_len=past_len, n_steps=n_steps),
        grid_spec=pltpu.PrefetchScalarGridSpec(
            num_scalar_prefetch=1, grid=(db, n_steps),
            in_specs=_page_specs(layer, n_pages) + [
                per_b(D_ATTN), per_b(KV_HEADS * LANE),
                pl.BlockSpec((None, 4, past_len // BLOCK, HEAD_DIM), lambda b, s, pt: (b, 0, 0, 0)),
                per_b(2 * HALF), per_b(HALF),
                pl.BlockSpec((ds * WIN_SLABS, HEAD_DIM), lambda b, s, pt: (b, 0)),
                pl.BlockSpec((None, None, WINDOW * WIN_SLABS, HEAD_DIM), lambda b, s, pt: (layer, b, 0, 0)),
                pl.BlockSpec(expand.shape, lambda b, s, pt: (0, 0))],
            out_specs=[per_b(D_ATTN),
                       pl.BlockSpec((None, WINDOW * WIN_SLABS, HEAD_DIM), lambda b, s, pt: (b, 0, 0))],
            scratch_shapes=[pltpu.VMEM((KV_HEADS, ROWS_PER_STEP, HEAD_DIM), BF16),
                            pltpu.VMEM((KV_HEADS, ROWS_PER_STEP, HEAD_DIM), BF16),
                            pltpu.VMEM((KV_HEADS, n_steps + 1, rows, LANE), BF16),
                            pltpu.VMEM((KV_HEADS, rows, HEAD_DIM), F32),
                            pltpu.VMEM((KV_HEADS, rows, 1), F32),
                            pltpu.VMEM((KV_HEADS, rows, 1), F32),
                            pltpu.VMEM((KV_HEADS, rows, HEAD_DIM), F32)]),
        out_shape=[jax.ShapeDtypeStruct((db * ds, D_ATTN), F32),
                   jax.ShapeDtypeStruct((db, WINDOW * WIN_SLABS, HEAD_DIM), F32)],
        compiler_params=_params(2),
        name="sample_attention",
    )(page_table, *([cache] * PAGES_PER_STEP), q, gates, kvc, kv4, kwin, kwin4, cache_win, expand)


def _gate_columns(w_gates):
    d = w_gates.shape[0]
    w = w_gates.reshape(d, 3, KV_HEADS, Q_PER_KV).transpose(0, 2, 1, 3).reshape(d, KV_HEADS, 3 * Q_PER_KV)
    return jnp.pad(w, ((0, 0), (0, 0), (0, LANE - 3 * Q_PER_KV))).reshape(d, KV_HEADS * LANE)


def kernel(x_prompt, x_sample, cache_kv, cache_win, state_conv, state_h, page_table, w_in, cmp_pe, cmp_w1, cmp_w2, conv_w, conv_b, lru_w_a, lru_b_a, lru_w_i, lru_b_i, lru_lam, group_norm, w_out, ln_g, ln_b, router_w, router_b, exp_w_gate, exp_w_up, exp_w_down):
    b, t, _ = x_prompt.shape
    db, ds, _ = x_sample.shape
    past_len = page_table.shape[1] * PAGE_SIZE
    n_p, n_s = b * t, db * ds
    tm_p, tm_moe, tq, tk, tt = 256, 512, 256, 1024, 256
    n_s_pad = -(-n_s // LANE) * LANE

    xp = x_prompt.reshape(n_p, D_MODEL)
    xs = x_sample.reshape(n_s, D_MODEL)
    expand = (jnp.arange(t)[None, :] // BLOCK == jnp.arange(t // BLOCK)[:, None]).astype(BF16)
    ltri = lambda m: (jnp.arange(m)[None, :] < jnp.arange(m)[:, None]).astype(BF16)
    ltri_p, ltri_s = ltri(tm_moe), ltri(n_s_pad)
    rw = jnp.pad(router_w, ((0, 0), (0, LANE - N_EXPERTS))).astype(BF16)
    rb = jnp.pad(router_b, (0, LANE - N_EXPERTS)).reshape(1, LANE)
    conv0_p = jnp.zeros((b, SUBLANE, D_RNN), F32)
    h0_p = jnp.zeros((b, SUBLANE, D_RNN), F32)
    pad_rows = SUBLANE - (CONV_WIDTH - 1)
    n_pages = page_table.shape[1]
    pt_flat = page_table.reshape(-1)
    cache = cache_kv.reshape(DEPTH, cache_kv.shape[1], PAGE_SIZE * ROW_SLABS, HEAD_DIM)
    cwin = cache_win.reshape(DEPTH, db, cache_win.shape[2] * WIN_SLABS, HEAD_DIM)
    expand_s = (jnp.arange(ROWS_PER_STEP)[None, :] // BLOCK == jnp.arange(LANE)[:, None]).astype(BF16)

    outs = [[] for _ in range(8)]
    for l in range(DEPTH):
        c_gate = 1024 + 6 * KV_HEADS * HEAD_DIM
        w_main = jnp.concatenate([w_in[l, :, :c_gate], w_in[l, :, c_gate + 3 * N_HEADS:]], axis=1).astype(BF16)
        w_gate = _gate_columns(w_in[l, :, c_gate:c_gate + 3 * N_HEADS]).astype(BF16)
        w1 = cmp_w1[l].astype(BF16)
        w2 = cmp_w2[l].astype(BF16)
        w_ai = jnp.concatenate([lru_w_a[l], lru_w_i[l]], axis=-1).astype(BF16)
        vec = lambda v: v.reshape(1, -1)
        lru = (conv_w[l], vec(conv_b[l]), w_ai, vec(lru_b_a[l]), vec(lru_b_i[l]), vec(lru_lam[l]))
        wo = w_out[l].astype(BF16)
        wg, wu, wd = exp_w_gate[l].astype(BF16), exp_w_up[l].astype(BF16), exp_w_down[l].astype(BF16)
        gn = vec(group_norm[l])

        q, kv4, kwin, kvb, gates, u, gg = _in_proj(xp, w_main, w_gate, tm_p)
        kvc = _compress(kv4, cmp_pe[l], w1, w2, b, t)
        attn = _prompt_attention(q, gates, kvc, kvb, expand, b, t, tq, tk)
        rnn, hl = _rglru(u, gg, conv0_p, h0_p, *lru, b, t, tt)
        x1, x1b, rt = _mixer(attn, rnn, xp, gn, wo, vec(ln_g[l, 0]), vec(ln_b[l, 0]), rw, rb, tm_p)
        y = _moe(x1b, rt, ltri_p, wg, wu, wd, tm_moe)
        xp = _ln2(x1, y, vec(ln_g[l, 1]), vec(ln_b[l, 1]), tm_p)
        outs[0].append(kv4.reshape(b, t, KV_SLOTS, KV_HEADS, HEAD_DIM))
        outs[2].append(kwin.reshape(b, t, 2, KV_HEADS, HEAD_DIM)[:, -min(WINDOW, t):])
        outs[4].append(u.reshape(b, t, D_RNN)[:, -(CONV_WIDTH - 1):])
        outs[6].append(hl.reshape(b, D_RNN))

        q, kv4, kwin, kvb, gates, u, gg = _in_proj(xs, w_main, w_gate, n_s)
        kvc = _sample_compress(pt_flat, cache, l, *_compress_operands(cmp_pe[l], cmp_w1[l]), w2, db, n_pages)
        attn, wnew = _sample_attention(pt_flat, cache, l, q.astype(F32), gates, kvc, kv4, kwin,
                                       kwin.reshape(n_s * WIN_SLABS, HEAD_DIM), cwin, expand_s, db, ds, n_pages)
        conv0_s = jnp.pad(state_conv[l], ((0, 0), (pad_rows, 0), (0, 0)))
        h0_s = jnp.pad(state_h[l][:, None, :], ((0, 0), (0, SUBLANE - 1), (0, 0)))
        rnn, hl = _rglru(u, gg, conv0_s, h0_s, *lru, db, ds, ds)
        x1, x1b, rt = _mixer(attn, rnn, xs, gn, wo, vec(ln_g[l, 0]), vec(ln_b[l, 0]),
                        rw, rb, n_s)
        x1p = jnp.pad(x1b, ((0, n_s_pad - n_s), (0, 0)))
        rtp = jnp.pad(rt, ((0, n_s_pad - n_s), (0, 0)), constant_values=-1.0)
        y = _moe_single_tile(x1p, rtp, ltri_s, wg, wu, wd)[:n_s]
        xs = _ln2(x1, y, vec(ln_g[l, 1]), vec(ln_b[l, 1]), n_s)
        outs[1].append(kv4.reshape(db, ds, KV_SLOTS, KV_HEADS, HEAD_DIM))
        outs[3].append(wnew.reshape(db, WINDOW, 2, KV_HEADS, HEAD_DIM))
        xpc = jnp.concatenate([state_conv[l], u.reshape(db, ds, D_RNN)], axis=1)
        outs[5].append(xpc[:, -(CONV_WIDTH - 1):])
        outs[7].append(hl.reshape(db, D_RNN))

    st = [jnp.stack(o) for o in outs]
    return (xp.reshape(b, t, D_MODEL), xs.reshape(db, ds, D_MODEL),
            st[0], st[1], st[2], st[3], st[4], st[5], st[6], st[7])
```

```python
import functools

import jax
import jax.numpy as jnp
from jax import lax
from jax.experimental import pallas as pl
from jax.experimental.pallas import tpu as pltpu

F32 = jnp.float32
BF16 = jnp.bfloat16
I32 = jnp.int32

D_MODEL = 2048
DEPTH = 4
PAGE_SIZE = 128
D_ATTN = D_MODEL // 2
D_RNN = D_MODEL - D_ATTN
N_HEADS = 8
HEAD_DIM = D_ATTN // N_HEADS
KV_HEADS = 2
Q_PER_KV = N_HEADS // KV_HEADS
BLOCK = 64
N_SELECT = 16
WINDOW = 512
FORCED_SCORE = 1e4
KV_SLOTS = 4
CONV_WIDTH = 4
LRU_BLOCKS = 8
LRU_BW = D_RNN // LRU_BLOCKS
LRU_C = 8.0
N_EXPERTS = 16
N_GROUPS = 4
EXPERTS_PER_GROUP = N_EXPERTS // N_GROUPS
D_EXPERT = 512
ALPHA = (2 * DEPTH) ** 0.25
LN_EPS = 1e-5
SCALE = HEAD_DIM ** -0.5
Q_SCALE = SCALE * 1.4426950408889634
NEG = -1e30

LANE = 128
SUBLANE = 8
MOE_CHUNK = 128
VMEM_LIMIT = 56 * 1024 * 1024

_NT = (((1,), (1,)), ((), ()))
_TN = (((0,), (0,)), ((), ()))


def _params(n_axes):
    return pltpu.CompilerParams(dimension_semantics=("arbitrary",) * n_axes,
                                vmem_limit_bytes=VMEM_LIMIT)


def _sigmoid(x):
    return 1.0 / (1.0 + jnp.exp(-x))


def _gelu(x):
    return 0.5 * x * (1.0 + jnp.tanh(0.7978845608028654 * (x + 0.044715 * (x * x * x))))


def _dot(a, b):
    return jnp.dot(a, b, preferred_element_type=F32)


def _resident(shape):
    nd = len(shape)
    return pl.BlockSpec(shape, lambda *_: (0,) * nd, pipeline_mode=pl.Buffered(1))


def _in_proj_kernel(x_ref, w_ref, wg_ref, q_ref, kv4_ref, kwin_ref, kvb_ref, gates_ref, u_ref, gg_ref):
    xb = x_ref[...].astype(BF16)

    def mm(lo, hi):
        return _dot(xb, w_ref[:, lo:hi])

    q_ref[...] = (mm(0, 1024) * Q_SCALE).astype(BF16)
    kv = mm(1024, 2048)
    kv4_ref[...] = kv
    kw = mm(2048, 2560)
    kwin_ref[...] = kw
    kvb_ref[:, 0:512] = kv[:, 512:1024].astype(BF16)
    kvb_ref[:, 512:1024] = kw.astype(BF16)
    gates_ref[...] = _sigmoid(_dot(xb, wg_ref[...]))
    u_ref[...] = mm(2560, 3584)
    gg_ref[...] = mm(3584, 4608)


def _in_proj(x, w_main, w_gate, tm):
    n = x.shape[0]
    row = lambda c: pl.BlockSpec((tm, c), lambda i: (i, 0))
    return pl.pallas_call(
        _in_proj_kernel,
        grid=(n // tm,),
        in_specs=[row(D_MODEL), _resident(w_main.shape), _resident(w_gate.shape)],
        out_specs=[row(1024), row(1024), row(512), row(1024), row(256), row(1024), row(1024)],
        out_shape=[jax.ShapeDtypeStruct((n, 1024), BF16),
                   jax.ShapeDtypeStruct((n, 1024), F32),
                   jax.ShapeDtypeStruct((n, 512), F32),
                   jax.ShapeDtypeStruct((n, 1024), BF16),
                   jax.ShapeDtypeStruct((n, 256), F32),
                   jax.ShapeDtypeStruct((n, 1024), F32),
                   jax.ShapeDtypeStruct((n, 1024), F32)],
        compiler_params=_params(1),
        name="in_proj",
    )(x, w_main, w_gate)


def _compress_kernel(x0_ref, x1_ref, x2_ref, x3_ref, pe_ref, w1_ref, w2_ref, o_ref, *, nblk):
    x_refs = (x0_ref, x1_ref, x2_ref, x3_ref)
    for s in range(2):
        acc = [jnp.zeros((nblk, HEAD_DIM), F32) for _ in range(KV_HEADS)]
        for l in range(BLOCK):
            pe = pe_ref[s, l:l + 1, :]
            w = w1_ref[s, l]
            for g in range(KV_HEADS):
                rows = x_refs[s * KV_HEADS + g][pl.ds(l, nblk, stride=BLOCK), :]
                acc[g] = acc[g] + _dot((rows + pe).astype(BF16), w)
        for g in range(KV_HEADS):
            h = _gelu(acc[g]).astype(BF16)
            o_ref[s * KV_HEADS + g] = _dot(h, w2_ref[s]).astype(BF16)


def _compress(kv4, pe, w1, w2, batch, t_len):
    nblk = t_len // BLOCK
    col = lambda c: pl.BlockSpec((t_len, HEAD_DIM), lambda b: (b, c))
    return pl.pallas_call(
        functools.partial(_compress_kernel, nblk=nblk),
        grid=(batch,),
        in_specs=[col(0), col(1), col(2), col(3),
                  _resident(pe.shape), _resident(w1.shape), _resident(w2.shape)],
        out_specs=pl.BlockSpec((None, 4, nblk, HEAD_DIM), lambda b: (b, 0, 0, 0)),
        out_shape=jax.ShapeDtypeStruct((batch, 4, nblk, HEAD_DIM), BF16),
        compiler_params=_params(1),
        name="compress",
    )(kv4, kv4, kv4, kv4, pe, w1, w2)


def _attn_kernel(q_ref, gates_ref, kc_ref, vc_ref, ks_ref, vs_ref, kw_ref, vw_ref, e_ref, o_ref,
                 *, tq, tk, t_len):
    i = pl.program_id(2)
    s0 = i * tq
    nblk = t_len // BLOCK
    qpos = s0 + lax.broadcasted_iota(I32, (tq, 1), 0)
    qh = [q_ref[:, r * HEAD_DIM:(r + 1) * HEAD_DIM] for r in range(Q_PER_KV)]

    blk = lax.broadcasted_iota(I32, (nblk, tq), 0)
    qpos_l = s0 + lax.broadcasted_iota(I32, (1, tq), 1)
    cmask = (blk + 1) * BLOCK - 1 <= qpos_l
    kc = kc_ref[...]
    vc = vc_ref[...]
    imp = jnp.zeros((nblk, tq), F32)
    o_cmp = []
    for r in range(Q_PER_KV):
        sc = jnp.where(cmask, lax.dot_general(kc, qh[r], _NT, preferred_element_type=F32), NEG)
        m = jnp.max(sc, axis=0, keepdims=True)
        e = jnp.where(cmask, jnp.exp2(sc - m), 0.0)
        p = e / jnp.maximum(jnp.sum(e, axis=0, keepdims=True), 1e-30)
        imp = imp + p
        o_cmp.append(lax.dot_general(p.astype(BF16), vc, _TN, preferred_element_type=F32))

    cur = jnp.right_shift(qpos_l, 6)
    forced = (blk == 0) | (blk == cur) | (blk == cur - 1)
    valid = blk <= cur
    score = jnp.where(valid, jnp.where(forced, FORCED_SCORE, imp), -1.0)
    rank = jnp.zeros((nblk, tq), I32)
    for j in range(nblk):
        cj = score[j:j + 1, :]
        beats = (cj > score) | ((cj == score) & (blk > j))
        rank = rank + beats.astype(I32)
    sel = ((rank < min(N_SELECT, nblk)) & valid).astype(BF16)

    def sel_body(kt, carry):
        off = pl.multiple_of(kt * tk, tk)
        k = ks_ref[pl.ds(off, tk), :]
        v = vs_ref[pl.ds(off, tk), :]
        hit = lax.dot_general(sel, e_ref[:, pl.ds(off, tk)], _TN, preferred_element_type=F32)
        kpos = off + lax.broadcasted_iota(I32, (tq, tk), 1)
        bias = jnp.where((hit > 0.5) & (kpos <= qpos), 0.0, NEG)
        out = []
        for r in range(Q_PER_KV):
            m, l, acc = carry[r]
            s = lax.dot_general(qh[r], k, _NT, preferred_element_type=F32) + bias
            m_new = jnp.maximum(m, jnp.max(s, axis=1, keepdims=True))
            a = jnp.exp2(m - m_new)
            p = jnp.exp2(s - m_new)
            l = a * l + jnp.sum(p, axis=1, keepdims=True)
            acc = a * acc + _dot(p.astype(BF16), v)
            out.append((m_new, l, acc))
        return tuple(out)

    init = tuple((jnp.full((tq, 1), NEG, F32), jnp.zeros((tq, 1), F32), jnp.zeros((tq, HEAD_DIM), F32))
                 for _ in range(Q_PER_KV))
    n_t = (s0 + tq + tk - 1) // tk
    fin = lax.fori_loop(0, n_t, sel_body, init)
    o_sel = [acc / l for (_, l, acc) in fin]

    wlen = WINDOW + tq
    start = pl.multiple_of(jnp.maximum(s0 - WINDOW, 0), tq)
    kw = kw_ref[pl.ds(start, wlen), :]
    vw = vw_ref[pl.ds(start, wlen), :]
    kposw = start + lax.broadcasted_iota(I32, (tq, wlen), 1)
    biasw = jnp.where((kposw <= qpos) & (kposw > qpos - WINDOW), 0.0, NEG)

    for r in range(Q_PER_KV):
        s = lax.dot_general(qh[r], kw, _NT, preferred_element_type=F32) + biasw
        p = jnp.exp2(s - jnp.max(s, axis=1, keepdims=True))
        o_win = _dot(p.astype(BF16), vw) / jnp.sum(p, axis=1, keepdims=True)
        g_cmp = gates_ref[:, r:r + 1]
        g_sel = gates_ref[:, Q_PER_KV + r:Q_PER_KV + r + 1]
        g_win = gates_ref[:, 2 * Q_PER_KV + r:2 * Q_PER_KV + r + 1]
        o_ref[:, r * HEAD_DIM:(r + 1) * HEAD_DIM] = g_cmp * o_cmp[r] + g_sel * o_sel[r] + g_win * o_win


def _prompt_attention(q, gates, kvc, kvb, expand, batch, t_len, tq, tk):
    nq = t_len // tq
    gw = Q_PER_KV * HEAD_DIM
    col = lambda c: pl.BlockSpec((t_len, HEAD_DIM), lambda b, g, i: (b, c + g))
    cmp_blk = lambda c: pl.BlockSpec((None, None, t_len // BLOCK, HEAD_DIM), lambda b, g, i: (b, c + g, 0, 0))
    return pl.pallas_call(
        functools.partial(_attn_kernel, tq=tq, tk=tk, t_len=t_len),
        grid=(batch, KV_HEADS, nq),
        in_specs=[pl.BlockSpec((tq, gw), lambda b, g, i: (b * nq + i, g)),
                  pl.BlockSpec((tq, LANE), lambda b, g, i: (b * nq + i, g)),
                  cmp_blk(0), cmp_blk(2),
                  col(0), col(2), col(4), col(6),
                  pl.BlockSpec(expand.shape, lambda b, g, i: (0, 0))],
        out_specs=pl.BlockSpec((tq, gw), lambda b, g, i: (b * nq + i, g)),
        out_shape=jax.ShapeDtypeStruct((batch * t_len, D_ATTN), F32),
        compiler_params=_params(3),
        name="prompt_attention",
    )(q, gates, kvc, kvc, kvb, kvb, kvb, kvb, expand)


def _rglru_kernel(u_ref, gg_ref, c0_ref, h0_ref, cw_ref, cb_ref, wai_ref, ba_ref, bi_ref, lam_ref,
                  y_ref, hl_ref, ext_ref, a_ref, b_ref, tail_ref, h_ref, *, tt):
    t = pl.program_id(1)

    @pl.when(t == 0)
    def _():
        tail_ref[...] = c0_ref[...]
        h_ref[...] = h0_ref[...]

    u = u_ref[...]
    ext_ref[0:SUBLANE, :] = tail_ref[...]
    ext_ref[SUBLANE:SUBLANE + tt, :] = u
    base = SUBLANE - (CONV_WIDTH - 1)
    xc = cb_ref[...] + ext_ref[base:base + tt, :] * cw_ref[0:1, :]
    for j in range(1, CONV_WIDTH - 1):
        xc = xc + ext_ref[base + j:base + j + tt, :] * cw_ref[j:j + 1, :]
    xc = xc + u * cw_ref[CONV_WIDTH - 1:CONV_WIDTH, :]
    tail_ref[...] = ext_ref[tt:tt + SUBLANE, :]

    xcb = xc.astype(BF16)
    for n in range(LRU_BLOCKS):
        c = slice(n * LRU_BW, (n + 1) * LRU_BW)
        ai = _dot(xcb[:, c], wai_ref[n])
        r = _sigmoid(ai[:, :LRU_BW] + ba_ref[:, c])
        ig = _sigmoid(ai[:, LRU_BW:] + bi_ref[:, c])
        nl = -lam_ref[:, c]
        softplus = jnp.maximum(nl, 0.0) + jnp.log1p(jnp.exp(-jnp.abs(nl)))
        log_a = (-LRU_C * softplus) * r
        a = jnp.exp(log_a)
        gain = jnp.sqrt(-jnp.tanh(log_a) * (a * a + 1.0))
        a_ref[:, c] = a
        b_ref[:, c] = gain * (ig * xc[:, c])

    def group(gi, h):
        off = pl.multiple_of(gi * SUBLANE, SUBLANE)
        a8 = a_ref[pl.ds(off, SUBLANE), :]
        b8 = b_ref[pl.ds(off, SUBLANE), :]
        rows = []
        for r in range(SUBLANE):
            h = a8[r:r + 1, :] * h + b8[r:r + 1, :]
            rows.append(h)
        a_ref[pl.ds(off, SUBLANE), :] = jnp.concatenate(rows, axis=0)
        return h

    h = lax.fori_loop(0, tt // SUBLANE, group, h_ref[0:1, :])
    h_ref[0:1, :] = h
    hl_ref[...] = h
    y_ref[...] = a_ref[...] * _gelu(gg_ref[...])


def _rglru(u, gg, conv0, h0, conv_w, conv_b, w_ai, b_a, b_i, lam, batch, t_len, tt):
    nt = t_len // tt
    c = D_RNN
    tile = pl.BlockSpec((tt, c), lambda b, t: (b * nt + t, 0))
    vec = _resident((1, c))
    return pl.pallas_call(
        functools.partial(_rglru_kernel, tt=tt),
        grid=(batch, nt),
        in_specs=[tile, tile,
                  pl.BlockSpec((None, SUBLANE, c), lambda b, t: (b, 0, 0)),
                  pl.BlockSpec((None, SUBLANE, c), lambda b, t: (b, 0, 0)),
                  _resident(conv_w.shape), vec, _resident(w_ai.shape), vec, vec, vec],
        out_specs=[tile, pl.BlockSpec((None, 1, c), lambda b, t: (b, 0, 0))],
        out_shape=[jax.ShapeDtypeStruct((batch * t_len, c), F32),
                   jax.ShapeDtypeStruct((batch, 1, c), F32)],
        scratch_shapes=[pltpu.VMEM((tt + SUBLANE, c), F32), pltpu.VMEM((tt, c), F32),
                        pltpu.VMEM((tt, c), F32), pltpu.VMEM((SUBLANE, c), F32),
                        pltpu.VMEM((SUBLANE, c), F32)],
        compiler_params=_params(2),
        name="rglru",
    )(u, gg, conv0, h0, conv_w, conv_b, w_ai, b_a, b_i, lam)


def _layer_norm(z, g, b):
    zc = z - jnp.mean(z, axis=1, keepdims=True)
    var = jnp.mean(zc * zc, axis=1, keepdims=True)
    return zc * lax.rsqrt(var + LN_EPS) * g + b


def _mixer_kernel(attn_ref, rnn_ref, x_ref, gn_ref, wo_ref, lng_ref, lnb_ref, rw_ref, rb_ref,
                  x1_ref, x1b_ref, rt_ref):
    def rms(v, g):
        return (v * lax.rsqrt(jnp.mean(v * v, axis=1, keepdims=True) + LN_EPS) * g).astype(BF16)

    an = rms(attn_ref[...], gn_ref[:, 0:D_ATTN])
    rn = rms(rnn_ref[...], gn_ref[:, D_ATTN:D_MODEL])
    y = _dot(an, wo_ref[0:D_ATTN, :]) + _dot(rn, wo_ref[D_ATTN:D_MODEL, :])
    x1 = _layer_norm(ALPHA * x_ref[...] + y, lng_ref[...], lnb_ref[...])
    x1_ref[...] = x1
    x1b = x1.astype(BF16)
    x1b_ref[...] = x1b

    s = _sigmoid(_dot(x1b, rw_ref[...]))
    biased = s + rb_ref[...]
    sc = [s[:, e:e + 1] for e in range(N_EXPERTS)]
    bc = [biased[:, e:e + 1] for e in range(N_EXPERTS)]
    best = grp = None
    for g in range(N_GROUPS):
        v = bc[g * EXPERTS_PER_GROUP:(g + 1) * EXPERTS_PER_GROUP]
        top2 = None
        for i in range(EXPERTS_PER_GROUP):
            for j in range(i + 1, EXPERTS_PER_GROUP):
                pair = v[i] + v[j]
                top2 = pair if top2 is None else jnp.maximum(top2, pair)
        if g == 0:
            best, grp = top2, jnp.zeros_like(top2, dtype=I32)
        else:
            better = top2 > best
            grp = jnp.where(better, g, grp)
            best = jnp.where(better, top2, best)

    def in_group(cols, j):
        out = cols[j]
        for g in range(1, N_GROUPS):
            out = jnp.where(grp == g, cols[g * EXPERTS_PER_GROUP + j], out)
        return out

    bv = [in_group(bc, j) for j in range(EXPERTS_PER_GROUP)]
    sv = [in_group(sc, j) for j in range(EXPERTS_PER_GROUP)]

    def argmax_first(vals, skip=None):
        bestv = idx = None
        for j, v in enumerate(vals):
            vj = v if skip is None else jnp.where(skip == j, -jnp.inf, v)
            if j == 0:
                bestv, idx = vj, jnp.zeros_like(grp)
            else:
                better = vj > bestv
                idx = jnp.where(better, j, idx)
                bestv = jnp.where(better, vj, bestv)
        return idx

    i1 = argmax_first(bv)
    i2 = argmax_first(bv, skip=i1)

    def pick(vals, idx):
        out = vals[0]
        for j in range(1, len(vals)):
            out = jnp.where(idx == j, vals[j], out)
        return out

    s1 = pick(sv, i1)
    s2 = pick(sv, i2)
    den = s1 + s2
    w1 = s1 / den
    w2 = s2 / den
    lane = lax.broadcasted_iota(I32, rt_ref.shape, 1)
    rt = jnp.where(lane == EXPERTS_PER_GROUP, grp.astype(F32), 0.0)
    rt = rt + jnp.where(lane == i1, w1, 0.0) + jnp.where(lane == i2, w2, 0.0)
    rt_ref[...] = rt


def _mixer(attn, rnn, x, gn, w_out, ln_g, ln_b, rw, rb, tm):
    n = x.shape[0]
    row = lambda c: pl.BlockSpec((tm, c), lambda i: (i, 0))
    vec = _resident((1, D_MODEL))
    return pl.pallas_call(
        _mixer_kernel,
        grid=(n // tm,),
        in_specs=[row(D_ATTN), row(D_RNN), row(D_MODEL), vec, _resident(w_out.shape), vec, vec,
                  _resident(rw.shape), _resident(rb.shape)],
        out_specs=[row(D_MODEL), row(D_MODEL), row(LANE)],
        out_shape=[jax.ShapeDtypeStruct((n, D_MODEL), F32),
                   jax.ShapeDtypeStruct((n, D_MODEL), BF16),
                   jax.ShapeDtypeStruct((n, LANE), F32)],
        compiler_params=_params(1),
        name="mixer_out",
    )(attn, rnn, x, gn, w_out, ln_g, ln_b, rw, rb)


def _split3(v):
    a = v.astype(BF16)
    r = v - a.astype(F32)
    b = r.astype(BF16)
    c = (r - b.astype(F32)).astype(BF16)
    return a, b, c


def _moe_kernel(*refs, tm, group, accumulate):
    if accumulate:
        x_ref, rt_ref, acc_ref, ltri_ref, wg_ref, wu_ref, wd_ref, o_ref = refs
    else:
        x_ref, rt_ref, ltri_ref, wg_ref, wu_ref, wd_ref, o_ref = refs
    g = pl.program_id(1) if group is None else group
    rt = rt_ref[...]
    member = rt[:, EXPERTS_PER_GROUP:EXPERTS_PER_GROUP + 1] == jnp.asarray(g, F32)
    mf = member.astype(F32)
    pos = _dot(ltri_ref[...], jnp.broadcast_to(mf, (tm, LANE)).astype(BF16))[:, 0:1].astype(I32)
    count = jnp.sum(mf).astype(I32)
    xb = x_ref[...]
    c1, c2, c3 = _split3(rt)
    lane = lax.broadcasted_iota(I32, (tm, MOE_CHUNK), 1)
    if accumulate:
        o_ref[...] = acc_ref[...]
    elif group is None:
        @pl.when(g == 0)
        def _():
            o_ref[...] = jnp.zeros_like(o_ref)
    else:
        o_ref[...] = jnp.zeros_like(o_ref)

    def chunk(k, carry):
        ptf = jnp.where(member & (pos - k * MOE_CHUNK == lane), 1.0, 0.0)
        pt = ptf.astype(BF16)
        p = ptf.T.astype(BF16)
        xc = _dot(p, xb).astype(BF16)
        cw = _dot(p, c1) + _dot(p, c2) + _dot(p, c3)
        y = jnp.zeros((MOE_CHUNK, D_MODEL), F32)
        for j in range(EXPERTS_PER_GROUP):
            hg = _dot(xc, wg_ref[j])
            hu = _dot(xc, wu_ref[j])
            h = (hg * _sigmoid(hg)) * hu * cw[:, j:j + 1]
            y = y + _dot(h.astype(BF16), wd_ref[j])
        yh = y.astype(BF16)
        yl = (y - yh.astype(F32)).astype(BF16)
        o_ref[...] += _dot(pt, yh) + _dot(pt, yl)
        return carry

    lax.fori_loop(0, (count + MOE_CHUNK - 1) // MOE_CHUNK, chunk, 0)


def _moe_single_tile(x, rt, ltri, wg, wu, wd):
    n = x.shape[0]
    row = lambda c: pl.BlockSpec((n, c), lambda t, g: (0, 0))
    grp = lambda a, b: pl.BlockSpec((EXPERTS_PER_GROUP, a, b), lambda t, g: (g, 0, 0))
    return pl.pallas_call(
        functools.partial(_moe_kernel, tm=n, group=None, accumulate=False),
        grid=(1, N_GROUPS),
        in_specs=[row(D_MODEL), row(LANE), pl.BlockSpec(ltri.shape, lambda t, g: (0, 0)),
                  grp(D_MODEL, D_EXPERT), grp(D_MODEL, D_EXPERT), grp(D_EXPERT, D_MODEL)],
        out_specs=row(D_MODEL),
        out_shape=jax.ShapeDtypeStruct((n, D_MODEL), F32),
        compiler_params=_params(2),
        name="grouped_moe_single_tile",
    )(x, rt, ltri, wg, wu, wd)


def _moe(x, rt, ltri, wg, wu, wd, tm):
    n = x.shape[0]
    row = lambda c: pl.BlockSpec((tm, c), lambda t: (t, 0))
    acc = None
    for g in range(N_GROUPS):
        grp = lambda a, b: pl.BlockSpec((EXPERTS_PER_GROUP, a, b), lambda t, g=g: (g, 0, 0),
                                        pipeline_mode=pl.Buffered(1))
        first = acc is None
        acc = pl.pallas_call(
            functools.partial(_moe_kernel, tm=tm, group=g, accumulate=not first),
            grid=(n // tm,),
            in_specs=[row(D_MODEL), row(LANE)] + ([] if first else [row(D_MODEL)]) + [
                _resident(ltri.shape), grp(D_MODEL, D_EXPERT), grp(D_MODEL, D_EXPERT), grp(D_EXPERT, D_MODEL)],
            out_specs=row(D_MODEL),
            out_shape=jax.ShapeDtypeStruct((n, D_MODEL), F32),
            input_output_aliases={} if first else {2: 0},
            compiler_params=_params(1),
            name="grouped_moe",
        )(*((x, rt) + (() if first else (acc,)) + (ltri, wg, wu, wd)))
    return acc


def _ln2_kernel(x_ref, y_ref, g_ref, b_ref, o_ref):
    o_ref[...] = _layer_norm(ALPHA * x_ref[...] + y_ref[...], g_ref[...], b_ref[...])


def _ln2(x, y, g, b, tm):
    n = x.shape[0]
    row = pl.BlockSpec((tm, D_MODEL), lambda i: (i, 0))
    vec = _resident((1, D_MODEL))
    return pl.pallas_call(
        _ln2_kernel, grid=(n // tm,), in_specs=[row, row, vec, vec], out_specs=row,
        out_shape=jax.ShapeDtypeStruct((n, D_MODEL), F32),
        compiler_params=_params(1), name="moe_residual_ln",
    )(x, y, g, b)


PAGES_PER_STEP = 32
ROWS_PER_STEP = PAGES_PER_STEP * PAGE_SIZE
BLOCKS_PER_STEP = ROWS_PER_STEP // BLOCK
HALF = KV_HEADS * HEAD_DIM * 2
ROW_SLABS = KV_SLOTS * KV_HEADS
WIN_SLABS = 2 * KV_HEADS


def _page_specs(layer, n_pages):
    def spec(j):
        return pl.BlockSpec((None, None, PAGE_SIZE * ROW_SLABS, HEAD_DIM),
                            lambda b, s, pt: (layer, pt[b * n_pages + s * PAGES_PER_STEP + j], 0, 0))
    return [spec(j) for j in range(PAGES_PER_STEP)]


def _slab(page, c):
    return page[pl.ds(c, PAGE_SIZE, stride=ROW_SLABS), :]


def _sample_compress_kernel(pt_ref, *refs):
    pages = refs[:PAGES_PER_STEP]
    pe_ref, w1_ref, w2_ref, o_ref, ksel_ref, vsel_ref, acc_ref = refs[PAGES_PER_STEP:]
    for j, page in enumerate(pages):
        rows = slice(j * PAGE_SIZE, (j + 1) * PAGE_SIZE)
        for g in range(KV_HEADS):
            ksel_ref[g, rows, :] = _slab(page, 2 * KV_HEADS + g).astype(BF16)
            vsel_ref[g, rows, :] = _slab(page, 3 * KV_HEADS + g).astype(BF16)
    m_rows = BLOCKS_PER_STEP * ROW_SLABS
    acc = jnp.zeros((m_rows, 2 * HEAD_DIM), F32)
    for lp in range(BLOCK // 2):
        parts = []
        for l in (2 * lp, 2 * lp + 1):
            tiles = [page[(h * BLOCK + l) * ROW_SLABS:(h * BLOCK + l + 1) * ROW_SLABS, :]
                     for page in pages for h in range(PAGE_SIZE // BLOCK)]
            pe = jnp.concatenate([pe_ref[l]] * BLOCKS_PER_STEP, axis=0)
            parts.append(jnp.concatenate(tiles, axis=0) + pe)
        lhs = jnp.concatenate(parts, axis=1).astype(BF16)
        acc = acc + _dot(lhs, w1_ref[lp])
    for s in range(2):
        acc_ref[s] = acc[:, s * HEAD_DIM:(s + 1) * HEAD_DIM]
    for s in range(2):
        for g in range(KV_HEADS):
            c = s * KV_HEADS + g
            h = _gelu(acc_ref[s, pl.ds(c, BLOCKS_PER_STEP, stride=ROW_SLABS), :]).astype(BF16)
            o_ref[c] = _dot(h, w2_ref[s]).astype(BF16)


def _sample_compress(page_table, cache, layer, pe_rows, w1_cat, w2, db, n_pages):
    n_steps = n_pages // PAGES_PER_STEP
    nblk = n_pages * PAGE_SIZE // BLOCK
    const = lambda shape: pl.BlockSpec(shape, lambda b, s, pt: (0,) * len(shape))
    sel_spec = pl.BlockSpec((None, KV_HEADS, ROWS_PER_STEP, HEAD_DIM), lambda b, s, pt: (b, 0, s, 0))
    sel_shape = jax.ShapeDtypeStruct((db, KV_HEADS, n_pages * PAGE_SIZE, HEAD_DIM), BF16)
    return pl.pallas_call(
        _sample_compress_kernel,
        grid_spec=pltpu.PrefetchScalarGridSpec(
            num_scalar_prefetch=1, grid=(db, n_steps),
            in_specs=_page_specs(layer, n_pages) + [const(pe_rows.shape), const(w1_cat.shape), const(w2.shape)],
            out_specs=[pl.BlockSpec((None, 4, BLOCKS_PER_STEP, HEAD_DIM), lambda b, s, pt: (b, 0, s, 0)),
                       sel_spec, sel_spec],
            scratch_shapes=[pltpu.VMEM((2, BLOCKS_PER_STEP * ROW_SLABS, HEAD_DIM), F32)]),
        out_shape=[jax.ShapeDtypeStruct((db, 4, nblk, HEAD_DIM), BF16), sel_shape, sel_shape],
        compiler_params=_params(2),
        name="sample_compress",
    )(page_table, *([cache] * PAGES_PER_STEP), pe_rows, w1_cat, w2)


def _compress_operands(pe, w1):
    pe_rows = jnp.zeros((BLOCK, ROW_SLABS, HEAD_DIM), F32)
    for s in range(2):
        for g in range(KV_HEADS):
            pe_rows = pe_rows.at[:, s * KV_HEADS + g, :].set(pe[s])
    w1_cat = w1.transpose(1, 2, 0, 3).reshape(BLOCK // 2, 2 * HEAD_DIM, 2 * HEAD_DIM).astype(BF16)
    return pe_rows, w1_cat


def _softmax_update(s, v, m, l, acc):
    m_new = jnp.maximum(m, jnp.max(s, axis=1, keepdims=True))
    a = jnp.exp2(m - m_new)
    p = jnp.exp2(s - m_new)
    return m_new, a * l + jnp.sum(p, axis=1, keepdims=True), a * acc + _dot(p.astype(BF16), v)


def _sample_attn_kernel(k_ref, v_ref, q_ref, gates_ref, kvc_ref, new_ref, nwin_ref, nwin4_ref, cwin_ref, e_ref,
                        o_ref, wout_ref, sel_ref, ocmp_ref, m_ref, l_ref, acc_ref, *, ds, past_len, n_steps):
    step = pl.program_id(1)
    rows = Q_PER_KV * ds
    nc = past_len // BLOCK
    ns = -(-(past_len + ds) // BLOCK)
    ns_pad = -(-ns // LANE) * LANE
    qi = lax.broadcasted_iota(I32, (rows, 1), 0) % ds
    qpos = past_len + qi

    def q_group(g):
        return jnp.concatenate([q_ref[:, (g * Q_PER_KV + r) * HEAD_DIM:(g * Q_PER_KV + r + 1) * HEAD_DIM]
                                for r in range(Q_PER_KV)], axis=0).astype(BF16)

    @pl.when(step == 0)
    def _():
        blk_c = lax.broadcasted_iota(I32, (rows, nc), 1)
        cmask = (blk_c + 1) * BLOCK - 1 <= qpos
        blk = lax.broadcasted_iota(I32, (ds, ns_pad), 1)
        qpos_q = past_len + lax.broadcasted_iota(I32, (ds, 1), 0)
        cur = qpos_q // BLOCK
        forced = (blk == 0) | (blk == cur) | (blk == cur - 1)
        valid = (blk <= cur) & (blk < ns)
        for g in range(KV_HEADS):
            qg = q_group(g)
            sc = jnp.where(cmask, lax.dot_general(qg, kvc_ref[g], _NT, preferred_element_type=F32), NEG)
            m = jnp.max(sc, axis=1, keepdims=True)
            e = jnp.where(cmask, jnp.exp2(sc - m), 0.0)
            p = e / jnp.maximum(jnp.sum(e, axis=1, keepdims=True), 1e-30)
            ocmp_ref[g] = _dot(p.astype(BF16), kvc_ref[KV_HEADS + g])
            imp = p[0:ds]
            for r in range(1, Q_PER_KV):
                imp = imp + p[r * ds:(r + 1) * ds]
            imp = jnp.concatenate([imp, jnp.zeros((ds, ns_pad - nc), F32)], axis=1)
            score = jnp.where(valid, jnp.where(forced, FORCED_SCORE, imp), -1.0)
            rank = jnp.zeros((ds, ns_pad), I32)
            for j in range(ns):
                cj = score[:, j:j + 1]
                rank = rank + ((cj > score) | ((cj == score) & (blk > j))).astype(I32)
            sel = ((rank < min(N_SELECT, ns)) & valid).astype(F32)
            sel = jnp.concatenate([sel] * Q_PER_KV, axis=0)
            for t in range(n_steps):
                part = sel[:, t * BLOCKS_PER_STEP:(t + 1) * BLOCKS_PER_STEP]
                sel_ref[g, t] = jnp.concatenate(
                    [part, jnp.zeros((rows, LANE - BLOCKS_PER_STEP), F32)], axis=1).astype(BF16)
            sel_ref[g, n_steps] = jnp.broadcast_to(sel[:, nc:nc + 1], (rows, LANE)).astype(BF16)
            m_ref[g] = jnp.full((rows, 1), NEG, F32)
            l_ref[g] = jnp.zeros((rows, 1), F32)
            acc_ref[g] = jnp.zeros((rows, HEAD_DIM), F32)

    for g in range(KV_HEADS):
        qg = q_group(g)
        hit = _dot(sel_ref[g, step], e_ref[...])
        s = lax.dot_general(qg, k_ref[g], _NT, preferred_element_type=F32)
        s = jnp.where(hit > 0.5, s, NEG)
        m, l, acc = _softmax_update(s, v_ref[g], m_ref[g], l_ref[g], acc_ref[g])
        m_ref[g] = m
        l_ref[g] = l
        acc_ref[g] = acc

    @pl.when(step == n_steps - 1)
    def _():
        pad = jnp.zeros((2 * SUBLANE - ds, HEAD_DIM), F32)
        col = lax.broadcasted_iota(I32, (rows, 2 * SUBLANE), 1)
        new_ok = (col <= qi) & (col < ds)
        wpos = lax.broadcasted_iota(I32, (rows, WINDOW), 1)
        for g in range(KV_HEADS):
            qg = q_group(g)
            kn = jnp.concatenate([new_ref[:, (2 * KV_HEADS + g) * HEAD_DIM:(2 * KV_HEADS + g + 1) * HEAD_DIM], pad], axis=0).astype(BF16)
            vn = jnp.concatenate([new_ref[:, (3 * KV_HEADS + g) * HEAD_DIM:(3 * KV_HEADS + g + 1) * HEAD_DIM], pad], axis=0).astype(BF16)
            s = lax.dot_general(qg, kn, _NT, preferred_element_type=F32)
            picked = sel_ref[g, n_steps][:, 0:2 * SUBLANE].astype(F32) > 0.5
            s = jnp.where(new_ok & picked, s, NEG)
            m, l, acc = _softmax_update(s, vn, m_ref[g], l_ref[g], acc_ref[g])
            o_sel = acc / l
            kw = cwin_ref[pl.ds(g, WINDOW, stride=WIN_SLABS), :].astype(BF16)
            vw = cwin_ref[pl.ds(KV_HEADS + g, WINDOW, stride=WIN_SLABS), :].astype(BF16)
            s = lax.dot_general(qg, kw, _NT, preferred_element_type=F32)
            s = jnp.where(wpos > qi, s, NEG)
            m, l, acc = _softmax_update(s, vw, jnp.full((rows, 1), NEG, F32), jnp.zeros((rows, 1), F32),
                                        jnp.zeros((rows, HEAD_DIM), F32))
            kn = jnp.concatenate([nwin_ref[:, g * HEAD_DIM:(g + 1) * HEAD_DIM], pad], axis=0).astype(BF16)
            vn = jnp.concatenate([nwin_ref[:, (KV_HEADS + g) * HEAD_DIM:(KV_HEADS + g + 1) * HEAD_DIM], pad], axis=0).astype(BF16)
            s = lax.dot_general(qg, kn, _NT, preferred_element_type=F32)
            s = jnp.where(new_ok, s, NEG)
            m, l, acc = _softmax_update(s, vn, m, l, acc)
            o_win = acc / l
            o_cmp = ocmp_ref[g]
            for r in range(Q_PER_KV):
                rs = slice(r * ds, (r + 1) * ds)
                c0 = g * LANE
                g_cmp = gates_ref[:, c0 + r:c0 + r + 1]
                g_sel = gates_ref[:, c0 + Q_PER_KV + r:c0 + Q_PER_KV + r + 1]
                g_win = gates_ref[:, c0 + 2 * Q_PER_KV + r:c0 + 2 * Q_PER_KV + r + 1]
                h = g * Q_PER_KV + r
                o_ref[:, h * HEAD_DIM:(h + 1) * HEAD_DIM] = g_cmp * o_cmp[rs] + g_sel * o_sel[rs] + g_win * o_win[rs]
        wout_ref[0:(WINDOW - ds) * WIN_SLABS, :] = cwin_ref[ds * WIN_SLABS:WINDOW * WIN_SLABS, :]
        wout_ref[(WINDOW - ds) * WIN_SLABS:WINDOW * WIN_SLABS, :] = nwin4_ref[...]


def _sample_attention(ksel, vsel, layer, q, gates, kvc, kv4, kwin, kwin4, cache_win, expand, db, ds):
    past_len = ksel.shape[2]
    n_steps = past_len // ROWS_PER_STEP
    assert ds == SUBLANE and past_len % ROWS_PER_STEP == 0 and past_len >= WINDOW
    assert cache_win.shape[2] == WINDOW * WIN_SLABS
    rows = Q_PER_KV * ds
    per_b = lambda c: pl.BlockSpec((ds, c), lambda b, s: (b, 0))
    sel_spec = pl.BlockSpec((None, KV_HEADS, ROWS_PER_STEP, HEAD_DIM), lambda b, s: (b, 0, s, 0))
    return pl.pallas_call(
        functools.partial(_sample_attn_kernel, ds=ds, past_len=past_len, n_steps=n_steps),
        grid=(db, n_steps),
        in_specs=[sel_spec, sel_spec,
                  per_b(D_ATTN), per_b(KV_HEADS * LANE),
                  pl.BlockSpec((None, 4, past_len // BLOCK, HEAD_DIM), lambda b, s: (b, 0, 0, 0)),
                  per_b(2 * HALF), per_b(HALF),
                  pl.BlockSpec((ds * WIN_SLABS, HEAD_DIM), lambda b, s: (b, 0)),
                  pl.BlockSpec((None, None, WINDOW * WIN_SLABS, HEAD_DIM), lambda b, s: (layer, b, 0, 0)),
                  pl.BlockSpec(expand.shape, lambda b, s: (0, 0))],
        out_specs=[per_b(D_ATTN),
                   pl.BlockSpec((None, WINDOW * WIN_SLABS, HEAD_DIM), lambda b, s: (b, 0, 0))],
        scratch_shapes=[pltpu.VMEM((KV_HEADS, n_steps + 1, rows, LANE), BF16),
                        pltpu.VMEM((KV_HEADS, rows, HEAD_DIM), F32),
                        pltpu.VMEM((KV_HEADS, rows, 1), F32),
                        pltpu.VMEM((KV_HEADS, rows, 1), F32),
                        pltpu.VMEM((KV_HEADS, rows, HEAD_DIM), F32)],
        out_shape=[jax.ShapeDtypeStruct((db * ds, D_ATTN), F32),
                   jax.ShapeDtypeStruct((db, WINDOW * WIN_SLABS, HEAD_DIM), F32)],
        compiler_params=_params(2),
        name="sample_attention",
    )(ksel, vsel, q, gates, kvc, kv4, kwin, kwin4, cache_win, expand)


def _gate_columns(w_gates):
    d = w_gates.shape[0]
    w = w_gates.reshape(d, 3, KV_HEADS, Q_PER_KV).transpose(0, 2, 1, 3).reshape(d, KV_HEADS, 3 * Q_PER_KV)
    return jnp.pad(w, ((0, 0), (0, 0), (0, LANE - 3 * Q_PER_KV))).reshape(d, KV_HEADS * LANE)


def kernel(x_prompt, x_sample, cache_kv, cache_win, state_conv, state_h, page_table, w_in, cmp_pe, cmp_w1, cmp_w2, conv_w, conv_b, lru_w_a, lru_b_a, lru_w_i, lru_b_i, lru_lam, group_norm, w_out, ln_g, ln_b, router_w, router_b, exp_w_gate, exp_w_up, exp_w_down):
    b, t, _ = x_prompt.shape
    db, ds, _ = x_sample.shape
    past_len = page_table.shape[1] * PAGE_SIZE
    n_p, n_s = b * t, db * ds
    tm_p, tm_moe, tq, tk, tt = 256, 512, 512, 1024, 256
    n_s_pad = -(-n_s // LANE) * LANE

    xp = x_prompt.reshape(n_p, D_MODEL)
    xs = x_sample.reshape(n_s, D_MODEL)
    expand = (jnp.arange(t)[None, :] // BLOCK == jnp.arange(t // BLOCK)[:, None]).astype(BF16)
    ltri = lambda m: (jnp.arange(m)[None, :] < jnp.arange(m)[:, None]).astype(BF16)
    ltri_p, ltri_s = ltri(tm_moe), ltri(n_s_pad)
    rw = jnp.pad(router_w, ((0, 0), (0, LANE - N_EXPERTS))).astype(BF16)
    rb = jnp.pad(router_b, (0, LANE - N_EXPERTS)).reshape(1, LANE)
    conv0_p = jnp.zeros((b, SUBLANE, D_RNN), F32)
    h0_p = jnp.zeros((b, SUBLANE, D_RNN), F32)
    pad_rows = SUBLANE - (CONV_WIDTH - 1)
    n_pages = page_table.shape[1]
    pt_flat = page_table.reshape(-1)
    cache = cache_kv.reshape(DEPTH, cache_kv.shape[1], PAGE_SIZE * ROW_SLABS, HEAD_DIM)
    cwin = cache_win.reshape(DEPTH, db, cache_win.shape[2] * WIN_SLABS, HEAD_DIM)
    expand_s = (jnp.arange(ROWS_PER_STEP)[None, :] // BLOCK == jnp.arange(LANE)[:, None]).astype(BF16)

    outs = [[] for _ in range(8)]
    for l in range(DEPTH):
        c_gate = 1024 + 6 * KV_HEADS * HEAD_DIM
        w_main = jnp.concatenate([w_in[l, :, :c_gate], w_in[l, :, c_gate + 3 * N_HEADS:]], axis=1).astype(BF16)
        w_gate = _gate_columns(w_in[l, :, c_gate:c_gate + 3 * N_HEADS]).astype(BF16)
        w1 = cmp_w1[l].astype(BF16)
        w2 = cmp_w2[l].astype(BF16)
        w_ai = jnp.concatenate([lru_w_a[l], lru_w_i[l]], axis=-1).astype(BF16)
        vec = lambda v: v.reshape(1, -1)
        lru = (conv_w[l], vec(conv_b[l]), w_ai, vec(lru_b_a[l]), vec(lru_b_i[l]), vec(lru_lam[l]))
        wo = w_out[l].astype(BF16)
        wg, wu, wd = exp_w_gate[l].astype(BF16), exp_w_up[l].astype(BF16), exp_w_down[l].astype(BF16)
        gn = vec(group_norm[l])

        q, kv4, kwin, kvb, gates, u, gg = _in_proj(xp, w_main, w_gate, tm_p)
        kvc = _compress(kv4, cmp_pe[l], w1, w2, b, t)
        attn = _prompt_attention(q, gates, kvc, kvb, expand, b, t, tq, tk)
        rnn, hl = _rglru(u, gg, conv0_p, h0_p, *lru, b, t, tt)
        x1, x1b, rt = _mixer(attn, rnn, xp, gn, wo, vec(ln_g[l, 0]), vec(ln_b[l, 0]), rw, rb, tm_p)
        y = _moe(x1b, rt, ltri_p, wg, wu, wd, tm_moe)
        xp = _ln2(x1, y, vec(ln_g[l, 1]), vec(ln_b[l, 1]), tm_p)
        outs[0].append(kv4.reshape(b, t, KV_SLOTS, KV_HEADS, HEAD_DIM))
        outs[2].append(kwin.reshape(b, t, 2, KV_HEADS, HEAD_DIM)[:, -min(WINDOW, t):])
        outs[4].append(u.reshape(b, t, D_RNN)[:, -(CONV_WIDTH - 1):])
        outs[6].append(hl.reshape(b, D_RNN))

        q, kv4, kwin, kvb, gates, u, gg = _in_proj(xs, w_main, w_gate, n_s)
        kvc, ksel, vsel = _sample_compress(pt_flat, cache, l, *_compress_operands(cmp_pe[l], cmp_w1[l]), w2,
                                           db, n_pages)
        attn, wnew = _sample_attention(ksel, vsel, l, q.astype(F32), gates, kvc, kv4, kwin,
                                       kwin.reshape(n_s * WIN_SLABS, HEAD_DIM), cwin, expand_s, db, ds)
        conv0_s = jnp.pad(state_conv[l], ((0, 0), (pad_rows, 0), (0, 0)))
        h0_s = jnp.pad(state_h[l][:, None, :], ((0, 0), (0, SUBLANE - 1), (0, 0)))
        rnn, hl = _rglru(u, gg, conv0_s, h0_s, *lru, db, ds, ds)
        x1, x1b, rt = _mixer(attn, rnn, xs, gn, wo, vec(ln_g[l, 0]), vec(ln_b[l, 0]),
                        rw, rb, n_s)
        x1p = jnp.pad(x1b, ((0, n_s_pad - n_s), (0, 0)))
        rtp = jnp.pad(rt, ((0, n_s_pad - n_s), (0, 0)), constant_values=-1.0)
        y = _moe_single_tile(x1p, rtp, ltri_s, wg, wu, wd)[:n_s]
        xs = _ln2(x1, y, vec(ln_g[l, 1]), vec(ln_b[l, 1]), n_s)
        outs[1].append(kv4.reshape(db, ds, KV_SLOTS, KV_HEADS, HEAD_DIM))
        outs[3].append(wnew.reshape(db, WINDOW, 2, KV_HEADS, HEAD_DIM))
        xpc = jnp.concatenate([state_conv[l], u.reshape(db, ds, D_RNN)], axis=1)
        outs[5].append(xpc[:, -(CONV_WIDTH - 1):])
        outs[7].append(hl.reshape(db, D_RNN))

    st = [jnp.stack(o) for o in outs]
    return (xp.reshape(b, t, D_MODEL), xs.reshape(db, ds, D_MODEL),
            st[0], st[1], st[2], st[3], st[4], st[5], st[6], st[7])
```

```python
import functools

import jax
import jax.numpy as jnp
from jax import lax
from jax.experimental import pallas as pl
from jax.experimental.pallas import tpu as pltpu

F32 = jnp.float32
BF16 = jnp.bfloat16
I32 = jnp.int32

D_MODEL = 2048
DEPTH = 4
PAGE_SIZE = 128
D_ATTN = D_MODEL // 2
D_RNN = D_MODEL - D_ATTN
N_HEADS = 8
HEAD_DIM = D_ATTN // N_HEADS
KV_HEADS = 2
Q_PER_KV = N_HEADS // KV_HEADS
BLOCK = 64
N_SELECT = 16
WINDOW = 512
FORCED_SCORE = 1e4
KV_SLOTS = 4
CONV_WIDTH = 4
LRU_BLOCKS = 8
LRU_BW = D_RNN // LRU_BLOCKS
LRU_C = 8.0
N_EXPERTS = 16
N_GROUPS = 4
EXPERTS_PER_GROUP = N_EXPERTS // N_GROUPS
D_EXPERT = 512
ALPHA = (2 * DEPTH) ** 0.25
LN_EPS = 1e-5
SCALE = HEAD_DIM ** -0.5
Q_SCALE = SCALE * 1.4426950408889634
NEG = -1e30

LANE = 128
SUBLANE = 8
MOE_CHUNK = 128
VMEM_LIMIT = 56 * 1024 * 1024

_NT = (((1,), (1,)), ((), ()))
_TN = (((0,), (0,)), ((), ()))


def _params(n_axes):
    return pltpu.CompilerParams(dimension_semantics=("arbitrary",) * n_axes,
                                vmem_limit_bytes=VMEM_LIMIT)


def _sigmoid(x):
    return 1.0 / (1.0 + jnp.exp(-x))


def _gelu(x):
    return 0.5 * x * (1.0 + jnp.tanh(0.7978845608028654 * (x + 0.044715 * (x * x * x))))


def _dot(a, b):
    return jnp.dot(a, b, preferred_element_type=F32)


def _resident(shape):
    nd = len(shape)
    return pl.BlockSpec(shape, lambda *_: (0,) * nd, pipeline_mode=pl.Buffered(1))


def _in_proj_kernel(x_ref, w_ref, wg_ref, q_ref, kv4_ref, kwin_ref, kvb_ref, gates_ref, u_ref, gg_ref):
    xb = x_ref[...].astype(BF16)

    def mm(lo, hi):
        return _dot(xb, w_ref[:, lo:hi])

    q_ref[...] = (mm(0, 1024) * Q_SCALE).astype(BF16)
    kv = mm(1024, 2048)
    kv4_ref[...] = kv
    kw = mm(2048, 2560)
    kwin_ref[...] = kw
    kvb_ref[:, 0:512] = kv[:, 512:1024].astype(BF16)
    kvb_ref[:, 512:1024] = kw.astype(BF16)
    gates_ref[...] = _sigmoid(_dot(xb, wg_ref[...]))
    u_ref[...] = mm(2560, 3584)
    gg_ref[...] = mm(3584, 4608)


def _in_proj(x, w_main, w_gate, tm):
    n = x.shape[0]
    row = lambda c: pl.BlockSpec((tm, c), lambda i: (i, 0))
    return pl.pallas_call(
        _in_proj_kernel,
        grid=(n // tm,),
        in_specs=[row(D_MODEL), _resident(w_main.shape), _resident(w_gate.shape)],
        out_specs=[row(1024), row(1024), row(512), row(1024), row(256), row(1024), row(1024)],
        out_shape=[jax.ShapeDtypeStruct((n, 1024), BF16),
                   jax.ShapeDtypeStruct((n, 1024), F32),
                   jax.ShapeDtypeStruct((n, 512), F32),
                   jax.ShapeDtypeStruct((n, 1024), BF16),
                   jax.ShapeDtypeStruct((n, 256), F32),
                   jax.ShapeDtypeStruct((n, 1024), F32),
                   jax.ShapeDtypeStruct((n, 1024), F32)],
        compiler_params=_params(1),
        name="in_proj",
    )(x, w_main, w_gate)


def _compress_kernel(x0_ref, x1_ref, x2_ref, x3_ref, pe_ref, w1_ref, w2_ref, o_ref, *, nblk):
    x_refs = (x0_ref, x1_ref, x2_ref, x3_ref)
    for s in range(2):
        acc = [jnp.zeros((nblk, HEAD_DIM), F32) for _ in range(KV_HEADS)]
        for l in range(BLOCK):
            pe = pe_ref[s, l:l + 1, :]
            w = w1_ref[s, l]
            for g in range(KV_HEADS):
                rows = x_refs[s * KV_HEADS + g][pl.ds(l, nblk, stride=BLOCK), :]
                acc[g] = acc[g] + _dot((rows + pe).astype(BF16), w)
        for g in range(KV_HEADS):
            h = _gelu(acc[g]).astype(BF16)
            o_ref[s * KV_HEADS + g] = _dot(h, w2_ref[s]).astype(BF16)


def _compress(kv4, pe, w1, w2, batch, t_len):
    nblk = t_len // BLOCK
    col = lambda c: pl.BlockSpec((t_len, HEAD_DIM), lambda b: (b, c))
    return pl.pallas_call(
        functools.partial(_compress_kernel, nblk=nblk),
        grid=(batch,),
        in_specs=[col(0), col(1), col(2), col(3),
                  _resident(pe.shape), _resident(w1.shape), _resident(w2.shape)],
        out_specs=pl.BlockSpec((None, 4, nblk, HEAD_DIM), lambda b: (b, 0, 0, 0)),
        out_shape=jax.ShapeDtypeStruct((batch, 4, nblk, HEAD_DIM), BF16),
        compiler_params=_params(1),
        name="compress",
    )(kv4, kv4, kv4, kv4, pe, w1, w2)


def _attn_kernel(q_ref, gates_ref, kc_ref, vc_ref, ks_ref, vs_ref, kw_ref, vw_ref, e_ref, o_ref,
                 *, tq, tk, t_len):
    i = pl.program_id(2)
    s0 = i * tq
    nblk = t_len // BLOCK
    qpos = s0 + lax.broadcasted_iota(I32, (tq, 1), 0)
    qh = [q_ref[:, r * HEAD_DIM:(r + 1) * HEAD_DIM] for r in range(Q_PER_KV)]

    blk = lax.broadcasted_iota(I32, (nblk, tq), 0)
    qpos_l = s0 + lax.broadcasted_iota(I32, (1, tq), 1)
    cmask = (blk + 1) * BLOCK - 1 <= qpos_l
    kc = kc_ref[...]
    vc = vc_ref[...]
    imp = jnp.zeros((nblk, tq), F32)
    o_cmp = []
    for r in range(Q_PER_KV):
        sc = jnp.where(cmask, lax.dot_general(kc, qh[r], _NT, preferred_element_type=F32), NEG)
        m = jnp.max(sc, axis=0, keepdims=True)
        e = jnp.where(cmask, jnp.exp2(sc - m), 0.0)
        p = e / jnp.maximum(jnp.sum(e, axis=0, keepdims=True), 1e-30)
        imp = imp + p
        o_cmp.append(lax.dot_general(p.astype(BF16), vc, _TN, preferred_element_type=F32))

    cur = jnp.right_shift(qpos_l, 6)
    forced = (blk == 0) | (blk == cur) | (blk == cur - 1)
    valid = blk <= cur
    score = jnp.where(valid, jnp.where(forced, FORCED_SCORE, imp), -1.0)
    rank = jnp.zeros((nblk, tq), I32)
    for j in range(nblk):
        cj = score[j:j + 1, :]
        beats = (cj > score) | ((cj == score) & (blk > j))
        rank = rank + beats.astype(I32)
    sel = ((rank < min(N_SELECT, nblk)) & valid).astype(BF16)

    def sel_body(kt, carry):
        off = pl.multiple_of(kt * tk, tk)
        k = ks_ref[pl.ds(off, tk), :]
        v = vs_ref[pl.ds(off, tk), :]
        hit = lax.dot_general(sel, e_ref[:, pl.ds(off, tk)], _TN, preferred_element_type=F32)
        kpos = off + lax.broadcasted_iota(I32, (tq, tk), 1)
        bias = jnp.where((hit > 0.5) & (kpos <= qpos), 0.0, NEG)
        out = []
        for r in range(Q_PER_KV):
            m, l, acc = carry[r]
            s = lax.dot_general(qh[r], k, _NT, preferred_element_type=F32) + bias
            m_new = jnp.maximum(m, jnp.max(s, axis=1, keepdims=True))
            a = jnp.exp2(m - m_new)
            p = jnp.exp2(s - m_new)
            l = a * l + jnp.sum(p, axis=1, keepdims=True)
            acc = a * acc + _dot(p.astype(BF16), v)
            out.append((m_new, l, acc))
        return tuple(out)

    init = tuple((jnp.full((tq, 1), NEG, F32), jnp.zeros((tq, 1), F32), jnp.zeros((tq, HEAD_DIM), F32))
                 for _ in range(Q_PER_KV))
    n_t = (s0 + tq + tk - 1) // tk
    fin = lax.fori_loop(0, n_t, sel_body, init)
    o_sel = [acc / l for (_, l, acc) in fin]

    wlen = WINDOW + tq
    start = pl.multiple_of(jnp.maximum(s0 - WINDOW, 0), tq)
    kw = kw_ref[pl.ds(start, wlen), :]
    vw = vw_ref[pl.ds(start, wlen), :]
    kposw = start + lax.broadcasted_iota(I32, (tq, wlen), 1)
    biasw = jnp.where((kposw <= qpos) & (kposw > qpos - WINDOW), 0.0, NEG)

    for r in range(Q_PER_KV):
        s = lax.dot_general(qh[r], kw, _NT, preferred_element_type=F32) + biasw
        p = jnp.exp2(s - jnp.max(s, axis=1, keepdims=True))
        o_win = _dot(p.astype(BF16), vw) / jnp.sum(p, axis=1, keepdims=True)
        g_cmp = gates_ref[:, r:r + 1]
        g_sel = gates_ref[:, Q_PER_KV + r:Q_PER_KV + r + 1]
        g_win = gates_ref[:, 2 * Q_PER_KV + r:2 * Q_PER_KV + r + 1]
        o_ref[:, r * HEAD_DIM:(r + 1) * HEAD_DIM] = g_cmp * o_cmp[r] + g_sel * o_sel[r] + g_win * o_win


def _prompt_attention(q, gates, kvc, kvb, expand, batch, t_len, tq, tk):
    nq = t_len // tq
    gw = Q_PER_KV * HEAD_DIM
    col = lambda c: pl.BlockSpec((t_len, HEAD_DIM), lambda b, g, i: (b, c + g))
    cmp_blk = lambda c: pl.BlockSpec((None, None, t_len // BLOCK, HEAD_DIM), lambda b, g, i: (b, c + g, 0, 0))
    return pl.pallas_call(
        functools.partial(_attn_kernel, tq=tq, tk=tk, t_len=t_len),
        grid=(batch, KV_HEADS, nq),
        in_specs=[pl.BlockSpec((tq, gw), lambda b, g, i: (b * nq + i, g)),
                  pl.BlockSpec((tq, LANE), lambda b, g, i: (b * nq + i, g)),
                  cmp_blk(0), cmp_blk(2),
                  col(0), col(2), col(4), col(6),
                  pl.BlockSpec(expand.shape, lambda b, g, i: (0, 0))],
        out_specs=pl.BlockSpec((tq, gw), lambda b, g, i: (b * nq + i, g)),
        out_shape=jax.ShapeDtypeStruct((batch * t_len, D_ATTN), F32),
        compiler_params=_params(3),
        name="prompt_attention",
    )(q, gates, kvc, kvc, kvb, kvb, kvb, kvb, expand)


def _rglru_kernel(u_ref, gg_ref, c0_ref, h0_ref, cw_ref, cb_ref, wai_ref, ba_ref, bi_ref, lam_ref,
                  y_ref, hl_ref, ext_ref, a_ref, b_ref, tail_ref, h_ref, *, tt):
    t = pl.program_id(1)

    @pl.when(t == 0)
    def _():
        tail_ref[...] = c0_ref[...]
        h_ref[...] = h0_ref[...]

    u = u_ref[...]
    ext_ref[0:SUBLANE, :] = tail_ref[...]
    ext_ref[SUBLANE:SUBLANE + tt, :] = u
    base = SUBLANE - (CONV_WIDTH - 1)
    xc = cb_ref[...] + ext_ref[base:base + tt, :] * cw_ref[0:1, :]
    for j in range(1, CONV_WIDTH - 1):
        xc = xc + ext_ref[base + j:base + j + tt, :] * cw_ref[j:j + 1, :]
    xc = xc + u * cw_ref[CONV_WIDTH - 1:CONV_WIDTH, :]
    tail_ref[...] = ext_ref[tt:tt + SUBLANE, :]

    xcb = xc.astype(BF16)
    for n in range(LRU_BLOCKS):
        c = slice(n * LRU_BW, (n + 1) * LRU_BW)
        ai = _dot(xcb[:, c], wai_ref[n])
        r = _sigmoid(ai[:, :LRU_BW] + ba_ref[:, c])
        ig = _sigmoid(ai[:, LRU_BW:] + bi_ref[:, c])
        nl = -lam_ref[:, c]
        softplus = jnp.maximum(nl, 0.0) + jnp.log1p(jnp.exp(-jnp.abs(nl)))
        log_a = (-LRU_C * softplus) * r
        a = jnp.exp(log_a)
        gain = jnp.sqrt(-jnp.tanh(log_a) * (a * a + 1.0))
        a_ref[:, c] = a
        b_ref[:, c] = gain * (ig * xc[:, c])

    def group(gi, h):
        off = pl.multiple_of(gi * SUBLANE, SUBLANE)
        a8 = a_ref[pl.ds(off, SUBLANE), :]
        b8 = b_ref[pl.ds(off, SUBLANE), :]
        rows = []
        for r in range(SUBLANE):
            h = a8[r:r + 1, :] * h + b8[r:r + 1, :]
            rows.append(h)
        a_ref[pl.ds(off, SUBLANE), :] = jnp.concatenate(rows, axis=0)
        return h

    h = lax.fori_loop(0, tt // SUBLANE, group, h_ref[0:1, :])
    h_ref[0:1, :] = h
    hl_ref[...] = h
    y_ref[...] = a_ref[...] * _gelu(gg_ref[...])


def _rglru(u, gg, conv0, h0, conv_w, conv_b, w_ai, b_a, b_i, lam, batch, t_len, tt):
    nt = t_len // tt
    c = D_RNN
    tile = pl.BlockSpec((tt, c), lambda b, t: (b * nt + t, 0))
    vec = _resident((1, c))
    return pl.pallas_call(
        functools.partial(_rglru_kernel, tt=tt),
        grid=(batch, nt),
        in_specs=[tile, tile,
                  pl.BlockSpec((None, SUBLANE, c), lambda b, t: (b, 0, 0)),
                  pl.BlockSpec((None, SUBLANE, c), lambda b, t: (b, 0, 0)),
                  _resident(conv_w.shape), vec, _resident(w_ai.shape), vec, vec, vec],
        out_specs=[tile, pl.BlockSpec((None, 1, c), lambda b, t: (b, 0, 0))],
        out_shape=[jax.ShapeDtypeStruct((batch * t_len, c), F32),
                   jax.ShapeDtypeStruct((batch, 1, c), F32)],
        scratch_shapes=[pltpu.VMEM((tt + SUBLANE, c), F32), pltpu.VMEM((tt, c), F32),
                        pltpu.VMEM((tt, c), F32), pltpu.VMEM((SUBLANE, c), F32),
                        pltpu.VMEM((SUBLANE, c), F32)],
        compiler_params=_params(2),
        name="rglru",
    )(u, gg, conv0, h0, conv_w, conv_b, w_ai, b_a, b_i, lam)


def _layer_norm(z, g, b):
    zc = z - jnp.mean(z, axis=1, keepdims=True)
    var = jnp.mean(zc * zc, axis=1, keepdims=True)
    return zc * lax.rsqrt(var + LN_EPS) * g + b


def _mixer_kernel(attn_ref, rnn_ref, x_ref, gn_ref, wo_ref, lng_ref, lnb_ref, rw_ref, rb_ref,
                  x1_ref, x1b_ref, rt_ref):
    def rms(v, g):
        return (v * lax.rsqrt(jnp.mean(v * v, axis=1, keepdims=True) + LN_EPS) * g).astype(BF16)

    an = rms(attn_ref[...], gn_ref[:, 0:D_ATTN])
    rn = rms(rnn_ref[...], gn_ref[:, D_ATTN:D_MODEL])
    y = _dot(an, wo_ref[0:D_ATTN, :]) + _dot(rn, wo_ref[D_ATTN:D_MODEL, :])
    x1 = _layer_norm(ALPHA * x_ref[...] + y, lng_ref[...], lnb_ref[...])
    x1_ref[...] = x1
    x1b = x1.astype(BF16)
    x1b_ref[...] = x1b

    s = _sigmoid(_dot(x1b, rw_ref[...]))
    biased = s + rb_ref[...]
    sc = [s[:, e:e + 1] for e in range(N_EXPERTS)]
    bc = [biased[:, e:e + 1] for e in range(N_EXPERTS)]
    best = grp = None
    for g in range(N_GROUPS):
        v = bc[g * EXPERTS_PER_GROUP:(g + 1) * EXPERTS_PER_GROUP]
        top2 = None
        for i in range(EXPERTS_PER_GROUP):
            for j in range(i + 1, EXPERTS_PER_GROUP):
                pair = v[i] + v[j]
                top2 = pair if top2 is None else jnp.maximum(top2, pair)
        if g == 0:
            best, grp = top2, jnp.zeros_like(top2, dtype=I32)
        else:
            better = top2 > best
            grp = jnp.where(better, g, grp)
            best = jnp.where(better, top2, best)

    def in_group(cols, j):
        out = cols[j]
        for g in range(1, N_GROUPS):
            out = jnp.where(grp == g, cols[g * EXPERTS_PER_GROUP + j], out)
        return out

    bv = [in_group(bc, j) for j in range(EXPERTS_PER_GROUP)]
    sv = [in_group(sc, j) for j in range(EXPERTS_PER_GROUP)]

    def argmax_first(vals, skip=None):
        bestv = idx = None
        for j, v in enumerate(vals):
            vj = v if skip is None else jnp.where(skip == j, -jnp.inf, v)
            if j == 0:
                bestv, idx = vj, jnp.zeros_like(grp)
            else:
                better = vj > bestv
                idx = jnp.where(better, j, idx)
                bestv = jnp.where(better, vj, bestv)
        return idx

    i1 = argmax_first(bv)
    i2 = argmax_first(bv, skip=i1)

    def pick(vals, idx):
        out = vals[0]
        for j in range(1, len(vals)):
            out = jnp.where(idx == j, vals[j], out)
        return out

    s1 = pick(sv, i1)
    s2 = pick(sv, i2)
    den = s1 + s2
    w1 = s1 / den
    w2 = s2 / den
    lane = lax.broadcasted_iota(I32, rt_ref.shape, 1)
    rt = jnp.where(lane == EXPERTS_PER_GROUP, grp.astype(F32), 0.0)
    rt = rt + jnp.where(lane == i1, w1, 0.0) + jnp.where(lane == i2, w2, 0.0)
    rt_ref[...] = rt


def _mixer(attn, rnn, x, gn, w_out, ln_g, ln_b, rw, rb, tm):
    n = x.shape[0]
    row = lambda c: pl.BlockSpec((tm, c), lambda i: (i, 0))
    vec = _resident((1, D_MODEL))
    return pl.pallas_call(
        _mixer_kernel,
        grid=(n // tm,),
        in_specs=[row(D_ATTN), row(D_RNN), row(D_MODEL), vec, _resident(w_out.shape), vec, vec,
                  _resident(rw.shape), _resident(rb.shape)],
        out_specs=[row(D_MODEL), row(D_MODEL), row(LANE)],
        out_shape=[jax.ShapeDtypeStruct((n, D_MODEL), F32),
                   jax.ShapeDtypeStruct((n, D_MODEL), BF16),
                   jax.ShapeDtypeStruct((n, LANE), F32)],
        compiler_params=_params(1),
        name="mixer_out",
    )(attn, rnn, x, gn, w_out, ln_g, ln_b, rw, rb)


def _split3(v):
    a = v.astype(BF16)
    r = v - a.astype(F32)
    b = r.astype(BF16)
    c = (r - b.astype(F32)).astype(BF16)
    return a, b, c


def _moe_kernel(*refs, tm, group, accumulate):
    if accumulate:
        x_ref, rt_ref, acc_ref, ltri_ref, wg_ref, wu_ref, wd_ref, o_ref = refs
    else:
        x_ref, rt_ref, ltri_ref, wg_ref, wu_ref, wd_ref, o_ref = refs
    g = pl.program_id(1) if group is None else group
    rt = rt_ref[...]
    member = rt[:, EXPERTS_PER_GROUP:EXPERTS_PER_GROUP + 1] == jnp.asarray(g, F32)
    mf = member.astype(F32)
    pos = _dot(ltri_ref[...], jnp.broadcast_to(mf, (tm, LANE)).astype(BF16))[:, 0:1].astype(I32)
    count = jnp.sum(mf).astype(I32)
    xb = x_ref[...]
    c1, c2, c3 = _split3(rt)
    if accumulate:
        o_ref[...] = acc_ref[...]
    elif group is None:
        @pl.when(g == 0)
        def _():
            o_ref[...] = jnp.zeros_like(o_ref)
    else:
        o_ref[...] = jnp.zeros_like(o_ref)

    def run_chunk(base, m_rows):
        lane = lax.broadcasted_iota(I32, (tm, m_rows), 1)
        ptf = jnp.where(member & (pos - base == lane), 1.0, 0.0)
        pt = ptf.astype(BF16)
        p = ptf.T.astype(BF16)
        xc = _dot(p, xb).astype(BF16)
        cw = _dot(p, c1) + _dot(p, c2) + _dot(p, c3)
        y = jnp.zeros((m_rows, D_MODEL), F32)
        for j in range(EXPERTS_PER_GROUP):
            hg = _dot(xc, wg_ref[j])
            hu = _dot(xc, wu_ref[j])
            h = (hg * _sigmoid(hg)) * hu * cw[:, j:j + 1]
            y = y + _dot(h.astype(BF16), wd_ref[j])
        yh = y.astype(BF16)
        yl = (y - yh.astype(F32)).astype(BF16)
        o_ref[...] += _dot(pt, yh) + _dot(pt, yl)

    @pl.when(count <= MOE_CHUNK)
    def _():
        @pl.when(count > 0)
        def _():
            run_chunk(0, MOE_CHUNK)

    @pl.when(count > MOE_CHUNK)
    def _():
        def big_chunk(k, carry):
            run_chunk(k * 2 * MOE_CHUNK, 2 * MOE_CHUNK)
            return carry
        lax.fori_loop(0, (count + 2 * MOE_CHUNK - 1) // (2 * MOE_CHUNK), big_chunk, 0)


def _moe_single_tile(x, rt, ltri, wg, wu, wd):
    n = x.shape[0]
    row = lambda c: pl.BlockSpec((n, c), lambda t, g: (0, 0))
    grp = lambda a, b: pl.BlockSpec((EXPERTS_PER_GROUP, a, b), lambda t, g: (g, 0, 0))
    return pl.pallas_call(
        functools.partial(_moe_kernel, tm=n, group=None, accumulate=False),
        grid=(1, N_GROUPS),
        in_specs=[row(D_MODEL), row(LANE), pl.BlockSpec(ltri.shape, lambda t, g: (0, 0)),
                  grp(D_MODEL, D_EXPERT), grp(D_MODEL, D_EXPERT), grp(D_EXPERT, D_MODEL)],
        out_specs=row(D_MODEL),
        out_shape=jax.ShapeDtypeStruct((n, D_MODEL), F32),
        compiler_params=_params(2),
        name="grouped_moe_single_tile",
    )(x, rt, ltri, wg, wu, wd)


def _moe(x, rt, ltri, wg, wu, wd, tm):
    n = x.shape[0]
    row = lambda c: pl.BlockSpec((tm, c), lambda t: (t, 0))
    acc = None
    for g in range(N_GROUPS):
        grp = lambda a, b: pl.BlockSpec((EXPERTS_PER_GROUP, a, b), lambda t, g=g: (g, 0, 0),
                                        pipeline_mode=pl.Buffered(1))
        first = acc is None
        acc = pl.pallas_call(
            functools.partial(_moe_kernel, tm=tm, group=g, accumulate=not first),
            grid=(n // tm,),
            in_specs=[row(D_MODEL), row(LANE)] + ([] if first else [row(D_MODEL)]) + [
                _resident(ltri.shape), grp(D_MODEL, D_EXPERT), grp(D_MODEL, D_EXPERT), grp(D_EXPERT, D_MODEL)],
            out_specs=row(D_MODEL),
            out_shape=jax.ShapeDtypeStruct((n, D_MODEL), F32),
            input_output_aliases={} if first else {2: 0},
            compiler_params=_params(1),
            name="grouped_moe",
        )(*((x, rt) + (() if first else (acc,)) + (ltri, wg, wu, wd)))
    return acc


def _ln2_kernel(x_ref, y_ref, g_ref, b_ref, o_ref):
    o_ref[...] = _layer_norm(ALPHA * x_ref[...] + y_ref[...], g_ref[...], b_ref[...])


def _ln2(x, y, g, b, tm):
    n = x.shape[0]
    row = pl.BlockSpec((tm, D_MODEL), lambda i: (i, 0))
    vec = _resident((1, D_MODEL))
    return pl.pallas_call(
        _ln2_kernel, grid=(n // tm,), in_specs=[row, row, vec, vec], out_specs=row,
        out_shape=jax.ShapeDtypeStruct((n, D_MODEL), F32),
        compiler_params=_params(1), name="moe_residual_ln",
    )(x, y, g, b)


PAGES_PER_STEP = 32
ROWS_PER_STEP = PAGES_PER_STEP * PAGE_SIZE
BLOCKS_PER_STEP = ROWS_PER_STEP // BLOCK
ATTN_BLOCKS_PER_STEP = LANE
ATTN_ROWS_PER_STEP = ATTN_BLOCKS_PER_STEP * BLOCK
HALF = KV_HEADS * HEAD_DIM * 2
ROW_SLABS = KV_SLOTS * KV_HEADS
WIN_SLABS = 2 * KV_HEADS


def _page_specs(layer, n_pages):
    def spec(j):
        return pl.BlockSpec((None, None, PAGE_SIZE * ROW_SLABS, HEAD_DIM),
                            lambda b, s, pt: (layer, pt[b * n_pages + s * PAGES_PER_STEP + j], 0, 0))
    return [spec(j) for j in range(PAGES_PER_STEP)]


def _slab(page, c):
    return page[pl.ds(c, PAGE_SIZE, stride=ROW_SLABS), :]


def _sample_compress_kernel(pt_ref, *refs):
    pages = refs[:PAGES_PER_STEP]
    pe_ref, w1_ref, w2_ref, o_ref, ksel_ref, vsel_ref, acc_ref = refs[PAGES_PER_STEP:]
    for j, page in enumerate(pages):
        rows = slice(j * PAGE_SIZE, (j + 1) * PAGE_SIZE)
        for g in range(KV_HEADS):
            ksel_ref[g, rows, :] = _slab(page, 2 * KV_HEADS + g).astype(BF16)
            vsel_ref[g, rows, :] = _slab(page, 3 * KV_HEADS + g).astype(BF16)
    m_rows = BLOCKS_PER_STEP * ROW_SLABS
    acc = jnp.zeros((m_rows, 2 * HEAD_DIM), F32)
    for lp in range(BLOCK // 2):
        parts = []
        for l in (2 * lp, 2 * lp + 1):
            tiles = [page[(h * BLOCK + l) * ROW_SLABS:(h * BLOCK + l + 1) * ROW_SLABS, :]
                     for page in pages for h in range(PAGE_SIZE // BLOCK)]
            pe = jnp.concatenate([pe_ref[l]] * BLOCKS_PER_STEP, axis=0)
            parts.append(jnp.concatenate(tiles, axis=0) + pe)
        lhs = jnp.concatenate(parts, axis=1).astype(BF16)
        acc = acc + _dot(lhs, w1_ref[lp])
    for s in range(2):
        acc_ref[s] = acc[:, s * HEAD_DIM:(s + 1) * HEAD_DIM]
    for s in range(2):
        for g in range(KV_HEADS):
            c = s * KV_HEADS + g
            h = _gelu(acc_ref[s, pl.ds(c, BLOCKS_PER_STEP, stride=ROW_SLABS), :]).astype(BF16)
            o_ref[c] = _dot(h, w2_ref[s]).astype(BF16)


def _sample_compress(page_table, cache, layer, pe_rows, w1_cat, w2, db, n_pages):
    n_steps = n_pages // PAGES_PER_STEP
    nblk = n_pages * PAGE_SIZE // BLOCK
    const = lambda shape: pl.BlockSpec(shape, lambda b, s, pt: (0,) * len(shape))
    sel_spec = pl.BlockSpec((None, KV_HEADS, ROWS_PER_STEP, HEAD_DIM), lambda b, s, pt: (b, 0, s, 0))
    sel_shape = jax.ShapeDtypeStruct((db, KV_HEADS, n_pages * PAGE_SIZE, HEAD_DIM), BF16)
    return pl.pallas_call(
        _sample_compress_kernel,
        grid_spec=pltpu.PrefetchScalarGridSpec(
            num_scalar_prefetch=1, grid=(db, n_steps),
            in_specs=_page_specs(layer, n_pages) + [const(pe_rows.shape), const(w1_cat.shape), const(w2.shape)],
            out_specs=[pl.BlockSpec((None, 4, BLOCKS_PER_STEP, HEAD_DIM), lambda b, s, pt: (b, 0, s, 0)),
                       sel_spec, sel_spec],
            scratch_shapes=[pltpu.VMEM((2, BLOCKS_PER_STEP * ROW_SLABS, HEAD_DIM), F32)]),
        out_shape=[jax.ShapeDtypeStruct((db, 4, nblk, HEAD_DIM), BF16), sel_shape, sel_shape],
        compiler_params=_params(2),
        name="sample_compress",
    )(page_table, *([cache] * PAGES_PER_STEP), pe_rows, w1_cat, w2)


def _compress_operands(pe, w1):
    pe_rows = jnp.zeros((BLOCK, ROW_SLABS, HEAD_DIM), F32)
    for s in range(2):
        for g in range(KV_HEADS):
            pe_rows = pe_rows.at[:, s * KV_HEADS + g, :].set(pe[s])
    w1_cat = w1.transpose(1, 2, 0, 3).reshape(BLOCK // 2, 2 * HEAD_DIM, 2 * HEAD_DIM).astype(BF16)
    return pe_rows, w1_cat


def _softmax_update(s, v, m, l, acc):
    m_new = jnp.maximum(m, jnp.max(s, axis=1, keepdims=True))
    a = jnp.exp2(m - m_new)
    p = jnp.exp2(s - m_new)
    return m_new, a * l + jnp.sum(p, axis=1, keepdims=True), a * acc + _dot(p.astype(BF16), v)


def _sample_attn_kernel(k_ref, v_ref, q_ref, gates_ref, kvc_ref, new_ref, nwin_ref, nwin4_ref, cwin_ref, e_ref,
                        o_ref, wout_ref, sel_ref, ocmp_ref, m_ref, l_ref, acc_ref, *, ds, past_len, n_steps):
    step = pl.program_id(1)
    rows = Q_PER_KV * ds
    nc = past_len // BLOCK
    ns = -(-(past_len + ds) // BLOCK)
    ns_pad = -(-ns // LANE) * LANE
    qi = lax.broadcasted_iota(I32, (rows, 1), 0) % ds
    qpos = past_len + qi

    def q_group(g):
        return jnp.concatenate([q_ref[:, (g * Q_PER_KV + r) * HEAD_DIM:(g * Q_PER_KV + r + 1) * HEAD_DIM]
                                for r in range(Q_PER_KV)], axis=0).astype(BF16)

    @pl.when(step == 0)
    def _():
        blk_c = lax.broadcasted_iota(I32, (rows, nc), 1)
        cmask = (blk_c + 1) * BLOCK - 1 <= qpos
        blk = lax.broadcasted_iota(I32, (ds, ns_pad), 1)
        qpos_q = past_len + lax.broadcasted_iota(I32, (ds, 1), 0)
        cur = qpos_q // BLOCK
        forced = (blk == 0) | (blk == cur) | (blk == cur - 1)
        valid = (blk <= cur) & (blk < ns)
        for g in range(KV_HEADS):
            qg = q_group(g)
            sc = jnp.where(cmask, lax.dot_general(qg, kvc_ref[g], _NT, preferred_element_type=F32), NEG)
            m = jnp.max(sc, axis=1, keepdims=True)
            e = jnp.where(cmask, jnp.exp2(sc - m), 0.0)
            p = e / jnp.maximum(jnp.sum(e, axis=1, keepdims=True), 1e-30)
            ocmp_ref[g] = _dot(p.astype(BF16), kvc_ref[KV_HEADS + g])
            imp = p[0:ds]
            for r in range(1, Q_PER_KV):
                imp = imp + p[r * ds:(r + 1) * ds]
            imp = jnp.concatenate([imp, jnp.zeros((ds, ns_pad - nc), F32)], axis=1)
            score = jnp.where(valid, jnp.where(forced, FORCED_SCORE, imp), -1.0)
            rank = jnp.zeros((ds, ns_pad), I32)
            for j in range(ns):
                cj = score[:, j:j + 1]
                rank = rank + ((cj > score) | ((cj == score) & (blk > j))).astype(I32)
            sel = ((rank < min(N_SELECT, ns)) & valid).astype(F32)
            sel = jnp.concatenate([sel] * Q_PER_KV, axis=0)
            for t in range(n_steps):
                sel_ref[g, t] = sel[:, t * ATTN_BLOCKS_PER_STEP:(t + 1) * ATTN_BLOCKS_PER_STEP].astype(BF16)
            sel_ref[g, n_steps] = jnp.broadcast_to(sel[:, nc:nc + 1], (rows, LANE)).astype(BF16)
            m_ref[g] = jnp.full((rows, 1), NEG, F32)
            l_ref[g] = jnp.zeros((rows, 1), F32)
            acc_ref[g] = jnp.zeros((rows, HEAD_DIM), F32)

    for g in range(KV_HEADS):
        qg = q_group(g)
        hit = _dot(sel_ref[g, step], e_ref[...])
        s = lax.dot_general(qg, k_ref[g], _NT, preferred_element_type=F32)
        s = jnp.where(hit > 0.5, s, NEG)
        m, l, acc = _softmax_update(s, v_ref[g], m_ref[g], l_ref[g], acc_ref[g])
        m_ref[g] = m
        l_ref[g] = l
        acc_ref[g] = acc

    @pl.when(step == n_steps - 1)
    def _():
        pad = jnp.zeros((2 * SUBLANE - ds, HEAD_DIM), F32)
        col = lax.broadcasted_iota(I32, (rows, 2 * SUBLANE), 1)
        new_ok = (col <= qi) & (col < ds)
        wpos = lax.broadcasted_iota(I32, (rows, WINDOW), 1)
        for g in range(KV_HEADS):
            qg = q_group(g)
            kn = jnp.concatenate([new_ref[:, (2 * KV_HEADS + g) * HEAD_DIM:(2 * KV_HEADS + g + 1) * HEAD_DIM], pad], axis=0).astype(BF16)
            vn = jnp.concatenate([new_ref[:, (3 * KV_HEADS + g) * HEAD_DIM:(3 * KV_HEADS + g + 1) * HEAD_DIM], pad], axis=0).astype(BF16)
            s = lax.dot_general(qg, kn, _NT, preferred_element_type=F32)
            picked = sel_ref[g, n_steps][:, 0:2 * SUBLANE].astype(F32) > 0.5
            s = jnp.where(new_ok & picked, s, NEG)
            m, l, acc = _softmax_update(s, vn, m_ref[g], l_ref[g], acc_ref[g])
            o_sel = acc / l
            kw = cwin_ref[pl.ds(g, WINDOW, stride=WIN_SLABS), :].astype(BF16)
            vw = cwin_ref[pl.ds(KV_HEADS + g, WINDOW, stride=WIN_SLABS), :].astype(BF16)
            s = lax.dot_general(qg, kw, _NT, preferred_element_type=F32)
            s = jnp.where(wpos > qi, s, NEG)
            m, l, acc = _softmax_update(s, vw, jnp.full((rows, 1), NEG, F32), jnp.zeros((rows, 1), F32),
                                        jnp.zeros((rows, HEAD_DIM), F32))
            kn = jnp.concatenate([nwin_ref[:, g * HEAD_DIM:(g + 1) * HEAD_DIM], pad], axis=0).astype(BF16)
            vn = jnp.concatenate([nwin_ref[:, (KV_HEADS + g) * HEAD_DIM:(KV_HEADS + g + 1) * HEAD_DIM], pad], axis=0).astype(BF16)
            s = lax.dot_general(qg, kn, _NT, preferred_element_type=F32)
            s = jnp.where(new_ok, s, NEG)
            m, l, acc = _softmax_update(s, vn, m, l, acc)
            o_win = acc / l
            o_cmp = ocmp_ref[g]
            for r in range(Q_PER_KV):
                rs = slice(r * ds, (r + 1) * ds)
                c0 = g * LANE
                g_cmp = gates_ref[:, c0 + r:c0 + r + 1]
                g_sel = gates_ref[:, c0 + Q_PER_KV + r:c0 + Q_PER_KV + r + 1]
                g_win = gates_ref[:, c0 + 2 * Q_PER_KV + r:c0 + 2 * Q_PER_KV + r + 1]
                h = g * Q_PER_KV + r
                o_ref[:, h * HEAD_DIM:(h + 1) * HEAD_DIM] = g_cmp * o_cmp[rs] + g_sel * o_sel[rs] + g_win * o_win[rs]
        wout_ref[0:(WINDOW - ds) * WIN_SLABS, :] = cwin_ref[ds * WIN_SLABS:WINDOW * WIN_SLABS, :]
        wout_ref[(WINDOW - ds) * WIN_SLABS:WINDOW * WIN_SLABS, :] = nwin4_ref[...]


def _sample_attention(ksel, vsel, layer, q, gates, kvc, kv4, kwin, kwin4, cache_win, expand, db, ds):
    past_len = ksel.shape[2]
    n_steps = past_len // ATTN_ROWS_PER_STEP
    assert ds == SUBLANE and past_len % ATTN_ROWS_PER_STEP == 0 and past_len >= WINDOW
    assert cache_win.shape[2] == WINDOW * WIN_SLABS
    rows = Q_PER_KV * ds
    per_b = lambda c: pl.BlockSpec((ds, c), lambda b, s: (b, 0))
    sel_spec = pl.BlockSpec((None, KV_HEADS, ATTN_ROWS_PER_STEP, HEAD_DIM), lambda b, s: (b, 0, s, 0))
    return pl.pallas_call(
        functools.partial(_sample_attn_kernel, ds=ds, past_len=past_len, n_steps=n_steps),
        grid=(db, n_steps),
        in_specs=[sel_spec, sel_spec,
                  per_b(D_ATTN), per_b(KV_HEADS * LANE),
                  pl.BlockSpec((None, 4, past_len // BLOCK, HEAD_DIM), lambda b, s: (b, 0, 0, 0)),
                  per_b(2 * HALF), per_b(HALF),
                  pl.BlockSpec((ds * WIN_SLABS, HEAD_DIM), lambda b, s: (b, 0)),
                  pl.BlockSpec((None, None, WINDOW * WIN_SLABS, HEAD_DIM), lambda b, s: (layer, b, 0, 0)),
                  pl.BlockSpec(expand.shape, lambda b, s: (0, 0))],
        out_specs=[per_b(D_ATTN),
                   pl.BlockSpec((None, WINDOW * WIN_SLABS, HEAD_DIM), lambda b, s: (b, 0, 0))],
        scratch_shapes=[pltpu.VMEM((KV_HEADS, n_steps + 1, rows, LANE), BF16),
                        pltpu.VMEM((KV_HEADS, rows, HEAD_DIM), F32),
                        pltpu.VMEM((KV_HEADS, rows, 1), F32),
                        pltpu.VMEM((KV_HEADS, rows, 1), F32),
                        pltpu.VMEM((KV_HEADS, rows, HEAD_DIM), F32)],
        out_shape=[jax.ShapeDtypeStruct((db * ds, D_ATTN), F32),
                   jax.ShapeDtypeStruct((db, WINDOW * WIN_SLABS, HEAD_DIM), F32)],
        compiler_params=_params(2),
        name="sample_attention",
    )(ksel, vsel, q, gates, kvc, kv4, kwin, kwin4, cache_win, expand)


def _gate_columns(w_gates):
    d = w_gates.shape[0]
    w = w_gates.reshape(d, 3, KV_HEADS, Q_PER_KV).transpose(0, 2, 1, 3).reshape(d, KV_HEADS, 3 * Q_PER_KV)
    return jnp.pad(w, ((0, 0), (0, 0), (0, LANE - 3 * Q_PER_KV))).reshape(d, KV_HEADS * LANE)


def kernel(x_prompt, x_sample, cache_kv, cache_win, state_conv, state_h, page_table, w_in, cmp_pe, cmp_w1, cmp_w2, conv_w, conv_b, lru_w_a, lru_b_a, lru_w_i, lru_b_i, lru_lam, group_norm, w_out, ln_g, ln_b, router_w, router_b, exp_w_gate, exp_w_up, exp_w_down):
    b, t, _ = x_prompt.shape
    db, ds, _ = x_sample.shape
    past_len = page_table.shape[1] * PAGE_SIZE
    n_p, n_s = b * t, db * ds
    tm_p, tm_moe, tq, tk, tt = 256, 512, 512, 1024, 256
    n_s_pad = -(-n_s // LANE) * LANE

    xp = x_prompt.reshape(n_p, D_MODEL)
    xs = x_sample.reshape(n_s, D_MODEL)
    expand = (jnp.arange(t)[None, :] // BLOCK == jnp.arange(t // BLOCK)[:, None]).astype(BF16)
    ltri = lambda m: (jnp.arange(m)[None, :] < jnp.arange(m)[:, None]).astype(BF16)
    ltri_p, ltri_s = ltri(tm_moe), ltri(n_s_pad)
    rw = jnp.pad(router_w, ((0, 0), (0, LANE - N_EXPERTS))).astype(BF16)
    rb = jnp.pad(router_b, (0, LANE - N_EXPERTS)).reshape(1, LANE)
    conv0_p = jnp.zeros((b, SUBLANE, D_RNN), F32)
    h0_p = jnp.zeros((b, SUBLANE, D_RNN), F32)
    pad_rows = SUBLANE - (CONV_WIDTH - 1)
    n_pages = page_table.shape[1]
    pt_flat = page_table.reshape(-1)
    cache = cache_kv.reshape(DEPTH, cache_kv.shape[1], PAGE_SIZE * ROW_SLABS, HEAD_DIM)
    cwin = cache_win.reshape(DEPTH, db, cache_win.shape[2] * WIN_SLABS, HEAD_DIM)
    expand_s = (jnp.arange(ATTN_ROWS_PER_STEP)[None, :] // BLOCK
                == jnp.arange(ATTN_BLOCKS_PER_STEP)[:, None]).astype(BF16)

    outs = [[] for _ in range(8)]
    for l in range(DEPTH):
        c_gate = 1024 + 6 * KV_HEADS * HEAD_DIM
        w_main = jnp.concatenate([w_in[l, :, :c_gate], w_in[l, :, c_gate + 3 * N_HEADS:]], axis=1).astype(BF16)
        w_gate = _gate_columns(w_in[l, :, c_gate:c_gate + 3 * N_HEADS]).astype(BF16)
        w1 = cmp_w1[l].astype(BF16)
        w2 = cmp_w2[l].astype(BF16)
        w_ai = jnp.concatenate([lru_w_a[l], lru_w_i[l]], axis=-1).astype(BF16)
        vec = lambda v: v.reshape(1, -1)
        lru = (conv_w[l], vec(conv_b[l]), w_ai, vec(lru_b_a[l]), vec(lru_b_i[l]), vec(lru_lam[l]))
        wo = w_out[l].astype(BF16)
        wg, wu, wd = exp_w_gate[l].astype(BF16), exp_w_up[l].astype(BF16), exp_w_down[l].astype(BF16)
        gn = vec(group_norm[l])

        q, kv4, kwin, kvb, gates, u, gg = _in_proj(xp, w_main, w_gate, tm_p)
        kvc = _compress(kv4, cmp_pe[l], w1, w2, b, t)
        attn = _prompt_attention(q, gates, kvc, kvb, expand, b, t, tq, tk)
        rnn, hl = _rglru(u, gg, conv0_p, h0_p, *lru, b, t, tt)
        x1, x1b, rt = _mixer(attn, rnn, xp, gn, wo, vec(ln_g[l, 0]), vec(ln_b[l, 0]), rw, rb, tm_p)
        y = _moe(x1b, rt, ltri_p, wg, wu, wd, tm_moe)
        xp = _ln2(x1, y, vec(ln_g[l, 1]), vec(ln_b[l, 1]), tm_p)
        outs[0].append(kv4.reshape(b, t, KV_SLOTS, KV_HEADS, HEAD_DIM))
        outs[2].append(kwin.reshape(b, t, 2, KV_HEADS, HEAD_DIM)[:, -min(WINDOW, t):])
        outs[4].append(u.reshape(b, t, D_RNN)[:, -(CONV_WIDTH - 1):])
        outs[6].append(hl.reshape(b, D_RNN))

        q, kv4, kwin, kvb, gates, u, gg = _in_proj(xs, w_main, w_gate, n_s)
        kvc, ksel, vsel = _sample_compress(pt_flat, cache, l, *_compress_operands(cmp_pe[l], cmp_w1[l]), w2,
                                           db, n_pages)
        attn, wnew = _sample_attention(ksel, vsel, l, q.astype(F32), gates, kvc, kv4, kwin,
                                       kwin.reshape(n_s * WIN_SLABS, HEAD_DIM), cwin, expand_s, db, ds)
        conv0_s = jnp.pad(state_conv[l], ((0, 0), (pad_rows, 0), (0, 0)))
        h0_s = jnp.pad(state_h[l][:, None, :], ((0, 0), (0, SUBLANE - 1), (0, 0)))
        rnn, hl = _rglru(u, gg, conv0_s, h0_s, *lru, db, ds, ds)
        x1, x1b, rt = _mixer(attn, rnn, xs, gn, wo, vec(ln_g[l, 0]), vec(ln_b[l, 0]),
                        rw, rb, n_s)
        x1p = jnp.pad(x1b, ((0, n_s_pad - n_s), (0, 0)))
        rtp = jnp.pad(rt, ((0, n_s_pad - n_s), (0, 0)), constant_values=-1.0)
        y = _moe_single_tile(x1p, rtp, ltri_s, wg, wu, wd)[:n_s]
        xs = _ln2(x1, y, vec(ln_g[l, 1]), vec(ln_b[l, 1]), n_s)
        outs[1].append(kv4.reshape(db, ds, KV_SLOTS, KV_HEADS, HEAD_DIM))
        outs[3].append(wnew.reshape(db, WINDOW, 2, KV_HEADS, HEAD_DIM))
        xpc = jnp.concatenate([state_conv[l], u.reshape(db, ds, D_RNN)], axis=1)
        outs[5].append(xpc[:, -(CONV_WIDTH - 1):])
        outs[7].append(hl.reshape(db, D_RNN))

    st = [jnp.stack(o) for o in outs]
    return (xp.reshape(b, t, D_MODEL), xs.reshape(db, ds, D_MODEL),
            st[0], st[1], st[2], st[3], st[4], st[5], st[6], st[7])
```

```python
import functools

import jax
import jax.numpy as jnp
from jax import lax
from jax.experimental import pallas as pl
from jax.experimental.pallas import tpu as pltpu

F32 = jnp.float32
BF16 = jnp.bfloat16
I32 = jnp.int32

D_MODEL = 2048
DEPTH = 4
PAGE_SIZE = 128
D_ATTN = D_MODEL // 2
D_RNN = D_MODEL - D_ATTN
N_HEADS = 8
HEAD_DIM = D_ATTN // N_HEADS
KV_HEADS = 2
Q_PER_KV = N_HEADS // KV_HEADS
BLOCK = 64
N_SELECT = 16
WINDOW = 512
FORCED_SCORE = 1e4
KV_SLOTS = 4
CONV_WIDTH = 4
LRU_BLOCKS = 8
LRU_BW = D_RNN // LRU_BLOCKS
LRU_C = 8.0
N_EXPERTS = 16
N_GROUPS = 4
EXPERTS_PER_GROUP = N_EXPERTS // N_GROUPS
D_EXPERT = 512
ALPHA = (2 * DEPTH) ** 0.25
LN_EPS = 1e-5
SCALE = HEAD_DIM ** -0.5
Q_SCALE = SCALE * 1.4426950408889634
NEG = -1e30

LANE = 128
SUBLANE = 8
MOE_CHUNK = 128
MOE_CHUNK_MID = 192
VMEM_LIMIT = 56 * 1024 * 1024

_NT = (((1,), (1,)), ((), ()))
_TN = (((0,), (0,)), ((), ()))


def _params(n_axes):
    return pltpu.CompilerParams(dimension_semantics=("arbitrary",) * n_axes,
                                vmem_limit_bytes=VMEM_LIMIT)


def _sigmoid(x):
    return 1.0 / (1.0 + jnp.exp(-x))


def _gelu(x):
    return 0.5 * x * (1.0 + jnp.tanh(0.7978845608028654 * (x + 0.044715 * (x * x * x))))


def _dot(a, b):
    return jnp.dot(a, b, preferred_element_type=F32)


def _resident(shape):
    nd = len(shape)
    return pl.BlockSpec(shape, lambda *_: (0,) * nd, pipeline_mode=pl.Buffered(1))


def _in_proj_kernel(x_ref, w_ref, wg_ref, q_ref, kv4_ref, kwin_ref, kvb_ref, gates_ref, u_ref, gg_ref):
    xb = x_ref[...].astype(BF16)

    def mm(lo, hi):
        return _dot(xb, w_ref[:, lo:hi])

    q_ref[...] = (mm(0, 1024) * Q_SCALE).astype(BF16)
    kv = mm(1024, 2048)
    kv4_ref[...] = kv
    kw = mm(2048, 2560)
    kwin_ref[...] = kw
    kvb_ref[:, 0:512] = kv[:, 512:1024].astype(BF16)
    kvb_ref[:, 512:1024] = kw.astype(BF16)
    gates_ref[...] = _sigmoid(_dot(xb, wg_ref[...]))
    u_ref[...] = mm(2560, 3584)
    gg_ref[...] = mm(3584, 4608)


def _in_proj(x, w_main, w_gate, tm):
    n = x.shape[0]
    row = lambda c: pl.BlockSpec((tm, c), lambda i: (i, 0))
    return pl.pallas_call(
        _in_proj_kernel,
        grid=(n // tm,),
        in_specs=[row(D_MODEL), _resident(w_main.shape), _resident(w_gate.shape)],
        out_specs=[row(1024), row(1024), row(512), row(1024), row(256), row(1024), row(1024)],
        out_shape=[jax.ShapeDtypeStruct((n, 1024), BF16),
                   jax.ShapeDtypeStruct((n, 1024), F32),
                   jax.ShapeDtypeStruct((n, 512), F32),
                   jax.ShapeDtypeStruct((n, 1024), BF16),
                   jax.ShapeDtypeStruct((n, 256), F32),
                   jax.ShapeDtypeStruct((n, 1024), F32),
                   jax.ShapeDtypeStruct((n, 1024), F32)],
        compiler_params=_params(1),
        name="in_proj",
    )(x, w_main, w_gate)


def _compress_kernel(x0_ref, x1_ref, x2_ref, x3_ref, pe_ref, w1_ref, w2_ref, o_ref, *, nblk):
    x_refs = (x0_ref, x1_ref, x2_ref, x3_ref)
    for s in range(2):
        acc = [jnp.zeros((nblk, HEAD_DIM), F32) for _ in range(KV_HEADS)]
        for l in range(BLOCK):
            pe = pe_ref[s, l:l + 1, :]
            w = w1_ref[s, l]
            for g in range(KV_HEADS):
                rows = x_refs[s * KV_HEADS + g][pl.ds(l, nblk, stride=BLOCK), :]
                acc[g] = acc[g] + _dot((rows + pe).astype(BF16), w)
        for g in range(KV_HEADS):
            h = _gelu(acc[g]).astype(BF16)
            o_ref[s * KV_HEADS + g] = _dot(h, w2_ref[s]).astype(BF16)


def _compress(kv4, pe, w1, w2, batch, t_len):
    nblk = t_len // BLOCK
    col = lambda c: pl.BlockSpec((t_len, HEAD_DIM), lambda b: (b, c))
    return pl.pallas_call(
        functools.partial(_compress_kernel, nblk=nblk),
        grid=(batch,),
        in_specs=[col(0), col(1), col(2), col(3),
                  _resident(pe.shape), _resident(w1.shape), _resident(w2.shape)],
        out_specs=pl.BlockSpec((None, 4, nblk, HEAD_DIM), lambda b: (b, 0, 0, 0)),
        out_shape=jax.ShapeDtypeStruct((batch, 4, nblk, HEAD_DIM), BF16),
        compiler_params=_params(1),
        name="compress",
    )(kv4, kv4, kv4, kv4, pe, w1, w2)


def _attn_kernel(q_ref, gates_ref, kc_ref, vc_ref, ks_ref, vs_ref, kw_ref, vw_ref, e_ref, o_ref,
                 *, tq, tk, t_len):
    i = pl.program_id(2)
    s0 = i * tq
    nblk = t_len // BLOCK
    qpos = s0 + lax.broadcasted_iota(I32, (tq, 1), 0)
    qh = [q_ref[:, r * HEAD_DIM:(r + 1) * HEAD_DIM] for r in range(Q_PER_KV)]

    blk = lax.broadcasted_iota(I32, (nblk, tq), 0)
    qpos_l = s0 + lax.broadcasted_iota(I32, (1, tq), 1)
    cmask = (blk + 1) * BLOCK - 1 <= qpos_l
    kc = kc_ref[...]
    vc = vc_ref[...]
    imp = jnp.zeros((nblk, tq), F32)
    o_cmp = []
    for r in range(Q_PER_KV):
        sc = jnp.where(cmask, lax.dot_general(kc, qh[r], _NT, preferred_element_type=F32), NEG)
        m = jnp.max(sc, axis=0, keepdims=True)
        e = jnp.where(cmask, jnp.exp2(sc - m), 0.0)
        p = e / jnp.maximum(jnp.sum(e, axis=0, keepdims=True), 1e-30)
        imp = imp + p
        o_cmp.append(lax.dot_general(p.astype(BF16), vc, _TN, preferred_element_type=F32))

    cur = jnp.right_shift(qpos_l, 6)
    forced = (blk == 0) | (blk == cur) | (blk == cur - 1)
    valid = blk <= cur
    score = jnp.where(valid, jnp.where(forced, FORCED_SCORE, imp), -1.0)
    rank = jnp.zeros((nblk, tq), I32)
    for j in range(nblk):
        cj = score[j:j + 1, :]
        beats = (cj > score) | ((cj == score) & (blk > j))
        rank = rank + beats.astype(I32)
    sel = ((rank < min(N_SELECT, nblk)) & valid).astype(BF16)

    def sel_body(kt, carry):
        off = pl.multiple_of(kt * tk, tk)
        k = ks_ref[pl.ds(off, tk), :]
        v = vs_ref[pl.ds(off, tk), :]
        hit = lax.dot_general(sel, e_ref[:, pl.ds(off, tk)], _TN, preferred_element_type=F32)
        kpos = off + lax.broadcasted_iota(I32, (tq, tk), 1)
        bias = jnp.where((hit > 0.5) & (kpos <= qpos), 0.0, NEG)
        out = []
        for r in range(Q_PER_KV):
            m, l, acc = carry[r]
            s = lax.dot_general(qh[r], k, _NT, preferred_element_type=F32) + bias
            m_new = jnp.maximum(m, jnp.max(s, axis=1, keepdims=True))
            a = jnp.exp2(m - m_new)
            p = jnp.exp2(s - m_new)
            l = a * l + jnp.sum(p, axis=1, keepdims=True)
            acc = a * acc + _dot(p.astype(BF16), v)
            out.append((m_new, l, acc))
        return tuple(out)

    init = tuple((jnp.full((tq, 1), NEG, F32), jnp.zeros((tq, 1), F32), jnp.zeros((tq, HEAD_DIM), F32))
                 for _ in range(Q_PER_KV))
    n_t = (s0 + tq + tk - 1) // tk
    fin = lax.fori_loop(0, n_t, sel_body, init)
    o_sel = [acc / l for (_, l, acc) in fin]

    wlen = WINDOW + tq
    start = pl.multiple_of(jnp.maximum(s0 - WINDOW, 0), tq)
    kw = kw_ref[pl.ds(start, wlen), :]
    vw = vw_ref[pl.ds(start, wlen), :]
    kposw = start + lax.broadcasted_iota(I32, (tq, wlen), 1)
    biasw = jnp.where((kposw <= qpos) & (kposw > qpos - WINDOW), 0.0, NEG)

    for r in range(Q_PER_KV):
        s = lax.dot_general(qh[r], kw, _NT, preferred_element_type=F32) + biasw
        p = jnp.exp2(s - jnp.max(s, axis=1, keepdims=True))
        o_win = _dot(p.astype(BF16), vw) / jnp.sum(p, axis=1, keepdims=True)
        g_cmp = gates_ref[:, r:r + 1]
        g_sel = gates_ref[:, Q_PER_KV + r:Q_PER_KV + r + 1]
        g_win = gates_ref[:, 2 * Q_PER_KV + r:2 * Q_PER_KV + r + 1]
        o_ref[:, r * HEAD_DIM:(r + 1) * HEAD_DIM] = g_cmp * o_cmp[r] + g_sel * o_sel[r] + g_win * o_win


def _prompt_attention(q, gates, kvc, kvb, expand, batch, t_len, tq, tk):
    nq = t_len // tq
    gw = Q_PER_KV * HEAD_DIM
    col = lambda c: pl.BlockSpec((t_len, HEAD_DIM), lambda b, g, i: (b, c + g))
    cmp_blk = lambda c: pl.BlockSpec((None, None, t_len // BLOCK, HEAD_DIM), lambda b, g, i: (b, c + g, 0, 0))
    return pl.pallas_call(
        functools.partial(_attn_kernel, tq=tq, tk=tk, t_len=t_len),
        grid=(batch, KV_HEADS, nq),
        in_specs=[pl.BlockSpec((tq, gw), lambda b, g, i: (b * nq + i, g)),
                  pl.BlockSpec((tq, LANE), lambda b, g, i: (b * nq + i, g)),
                  cmp_blk(0), cmp_blk(2),
                  col(0), col(2), col(4), col(6),
                  pl.BlockSpec(expand.shape, lambda b, g, i: (0, 0))],
        out_specs=pl.BlockSpec((tq, gw), lambda b, g, i: (b * nq + i, g)),
        out_shape=jax.ShapeDtypeStruct((batch * t_len, D_ATTN), F32),
        compiler_params=_params(3),
        name="prompt_attention",
    )(q, gates, kvc, kvc, kvb, kvb, kvb, kvb, expand)


def _rglru_kernel(u_ref, gg_ref, c0_ref, h0_ref, cw_ref, cb_ref, wai_ref, ba_ref, bi_ref, lam_ref,
                  y_ref, hl_ref, ext_ref, a_ref, b_ref, tail_ref, h_ref, *, tt):
    t = pl.program_id(1)

    @pl.when(t == 0)
    def _():
        tail_ref[...] = c0_ref[...]
        h_ref[...] = h0_ref[...]

    u = u_ref[...]
    ext_ref[0:SUBLANE, :] = tail_ref[...]
    ext_ref[SUBLANE:SUBLANE + tt, :] = u
    base = SUBLANE - (CONV_WIDTH - 1)
    xc = cb_ref[...] + ext_ref[base:base + tt, :] * cw_ref[0:1, :]
    for j in range(1, CONV_WIDTH - 1):
        xc = xc + ext_ref[base + j:base + j + tt, :] * cw_ref[j:j + 1, :]
    xc = xc + u * cw_ref[CONV_WIDTH - 1:CONV_WIDTH, :]
    tail_ref[...] = ext_ref[tt:tt + SUBLANE, :]

    xcb = xc.astype(BF16)
    for n in range(LRU_BLOCKS):
        c = slice(n * LRU_BW, (n + 1) * LRU_BW)
        ai = _dot(xcb[:, c], wai_ref[n])
        r = _sigmoid(ai[:, :LRU_BW] + ba_ref[:, c])
        ig = _sigmoid(ai[:, LRU_BW:] + bi_ref[:, c])
        nl = -lam_ref[:, c]
        softplus = jnp.maximum(nl, 0.0) + jnp.log1p(jnp.exp(-jnp.abs(nl)))
        log_a = (-LRU_C * softplus) * r
        a = jnp.exp(log_a)
        gain = jnp.sqrt(-jnp.tanh(log_a) * (a * a + 1.0))
        a_ref[:, c] = a
        b_ref[:, c] = gain * (ig * xc[:, c])

    def group(gi, h):
        off = pl.multiple_of(gi * SUBLANE, SUBLANE)
        a8 = a_ref[pl.ds(off, SUBLANE), :]
        b8 = b_ref[pl.ds(off, SUBLANE), :]
        rows = []
        for r in range(SUBLANE):
            h = a8[r:r + 1, :] * h + b8[r:r + 1, :]
            rows.append(h)
        a_ref[pl.ds(off, SUBLANE), :] = jnp.concatenate(rows, axis=0)
        return h

    h = lax.fori_loop(0, tt // SUBLANE, group, h_ref[0:1, :])
    h_ref[0:1, :] = h
    hl_ref[...] = h
    y_ref[...] = a_ref[...] * _gelu(gg_ref[...])


def _rglru(u, gg, conv0, h0, conv_w, conv_b, w_ai, b_a, b_i, lam, batch, t_len, tt):
    nt = t_len // tt
    c = D_RNN
    tile = pl.BlockSpec((tt, c), lambda b, t: (b * nt + t, 0))
    vec = _resident((1, c))
    return pl.pallas_call(
        functools.partial(_rglru_kernel, tt=tt),
        grid=(batch, nt),
        in_specs=[tile, tile,
                  pl.BlockSpec((None, SUBLANE, c), lambda b, t: (b, 0, 0)),
                  pl.BlockSpec((None, SUBLANE, c), lambda b, t: (b, 0, 0)),
                  _resident(conv_w.shape), vec, _resident(w_ai.shape), vec, vec, vec],
        out_specs=[tile, pl.BlockSpec((None, 1, c), lambda b, t: (b, 0, 0))],
        out_shape=[jax.ShapeDtypeStruct((batch * t_len, c), F32),
                   jax.ShapeDtypeStruct((batch, 1, c), F32)],
        scratch_shapes=[pltpu.VMEM((tt + SUBLANE, c), F32), pltpu.VMEM((tt, c), F32),
                        pltpu.VMEM((tt, c), F32), pltpu.VMEM((SUBLANE, c), F32),
                        pltpu.VMEM((SUBLANE, c), F32)],
        compiler_params=_params(2),
        name="rglru",
    )(u, gg, conv0, h0, conv_w, conv_b, w_ai, b_a, b_i, lam)


def _layer_norm(z, g, b):
    zc = z - jnp.mean(z, axis=1, keepdims=True)
    var = jnp.mean(zc * zc, axis=1, keepdims=True)
    return zc * lax.rsqrt(var + LN_EPS) * g + b


def _mixer_kernel(attn_ref, rnn_ref, x_ref, gn_ref, wo_ref, lng_ref, lnb_ref, rw_ref, rb_ref,
                  x1_ref, x1b_ref, rt_ref):
    def rms(v, g):
        return (v * lax.rsqrt(jnp.mean(v * v, axis=1, keepdims=True) + LN_EPS) * g).astype(BF16)

    an = rms(attn_ref[...], gn_ref[:, 0:D_ATTN])
    rn = rms(rnn_ref[...], gn_ref[:, D_ATTN:D_MODEL])
    y = _dot(an, wo_ref[0:D_ATTN, :]) + _dot(rn, wo_ref[D_ATTN:D_MODEL, :])
    x1 = _layer_norm(ALPHA * x_ref[...] + y, lng_ref[...], lnb_ref[...])
    x1_ref[...] = x1
    x1b = x1.astype(BF16)
    x1b_ref[...] = x1b

    s = _sigmoid(_dot(x1b, rw_ref[...]))
    biased = s + rb_ref[...]
    sc = [s[:, e:e + 1] for e in range(N_EXPERTS)]
    bc = [biased[:, e:e + 1] for e in range(N_EXPERTS)]
    best = grp = None
    for g in range(N_GROUPS):
        v = bc[g * EXPERTS_PER_GROUP:(g + 1) * EXPERTS_PER_GROUP]
        top2 = None
        for i in range(EXPERTS_PER_GROUP):
            for j in range(i + 1, EXPERTS_PER_GROUP):
                pair = v[i] + v[j]
                top2 = pair if top2 is None else jnp.maximum(top2, pair)
        if g == 0:
            best, grp = top2, jnp.zeros_like(top2, dtype=I32)
        else:
            better = top2 > best
            grp = jnp.where(better, g, grp)
            best = jnp.where(better, top2, best)

    def in_group(cols, j):
        out = cols[j]
        for g in range(1, N_GROUPS):
            out = jnp.where(grp == g, cols[g * EXPERTS_PER_GROUP + j], out)
        return out

    bv = [in_group(bc, j) for j in range(EXPERTS_PER_GROUP)]
    sv = [in_group(sc, j) for j in range(EXPERTS_PER_GROUP)]

    def argmax_first(vals, skip=None):
        bestv = idx = None
        for j, v in enumerate(vals):
            vj = v if skip is None else jnp.where(skip == j, -jnp.inf, v)
            if j == 0:
                bestv, idx = vj, jnp.zeros_like(grp)
            else:
                better = vj > bestv
                idx = jnp.where(better, j, idx)
                bestv = jnp.where(better, vj, bestv)
        return idx

    i1 = argmax_first(bv)
    i2 = argmax_first(bv, skip=i1)

    def pick(vals, idx):
        out = vals[0]
        for j in range(1, len(vals)):
            out = jnp.where(idx == j, vals[j], out)
        return out

    s1 = pick(sv, i1)
    s2 = pick(sv, i2)
    den = s1 + s2
    w1 = s1 / den
    w2 = s2 / den
    lane = lax.broadcasted_iota(I32, rt_ref.shape, 1)
    rt = jnp.where(lane == EXPERTS_PER_GROUP, grp.astype(F32), 0.0)
    rt = rt + jnp.where(lane == i1, w1, 0.0) + jnp.where(lane == i2, w2, 0.0)
    rt_ref[...] = rt


def _mixer(attn, rnn, x, gn, w_out, ln_g, ln_b, rw, rb, tm):
    n = x.shape[0]
    row = lambda c: pl.BlockSpec((tm, c), lambda i: (i, 0))
    vec = _resident((1, D_MODEL))
    return pl.pallas_call(
        _mixer_kernel,
        grid=(n // tm,),
        in_specs=[row(D_ATTN), row(D_RNN), row(D_MODEL), vec, _resident(w_out.shape), vec, vec,
                  _resident(rw.shape), _resident(rb.shape)],
        out_specs=[row(D_MODEL), row(D_MODEL), row(LANE)],
        out_shape=[jax.ShapeDtypeStruct((n, D_MODEL), F32),
                   jax.ShapeDtypeStruct((n, D_MODEL), BF16),
                   jax.ShapeDtypeStruct((n, LANE), F32)],
        compiler_params=_params(1),
        name="mixer_out",
    )(attn, rnn, x, gn, w_out, ln_g, ln_b, rw, rb)


def _split3(v):
    a = v.astype(BF16)
    r = v - a.astype(F32)
    b = r.astype(BF16)
    c = (r - b.astype(F32)).astype(BF16)
    return a, b, c


def _moe_kernel(*refs, tm, group, accumulate):
    if accumulate:
        x_ref, rt_ref, acc_ref, ltri_ref, wg_ref, wu_ref, wd_ref, o_ref = refs
    else:
        x_ref, rt_ref, ltri_ref, wg_ref, wu_ref, wd_ref, o_ref = refs
    g = pl.program_id(1) if group is None else group
    rt = rt_ref[...]
    member = rt[:, EXPERTS_PER_GROUP:EXPERTS_PER_GROUP + 1] == jnp.asarray(g, F32)
    mf = member.astype(F32)
    pos = _dot(ltri_ref[...], jnp.broadcast_to(mf, (tm, LANE)).astype(BF16))[:, 0:1].astype(I32)
    count = jnp.sum(mf).astype(I32)
    xb = x_ref[...]
    c1, c2, c3 = _split3(rt)
    if accumulate:
        o_ref[...] = acc_ref[...]
    elif group is None:
        @pl.when(g == 0)
        def _():
            o_ref[...] = jnp.zeros_like(o_ref)
    else:
        o_ref[...] = jnp.zeros_like(o_ref)

    def run_chunk(base, m_rows):
        m_pad = -(-m_rows // LANE) * LANE
        lane = lax.broadcasted_iota(I32, (tm, m_pad), 1)
        hit = member & (pos - base == lane)
        if m_pad > m_rows:
            hit = hit & (lane < m_rows)
        ptf = jnp.where(hit, 1.0, 0.0)
        pt = ptf.astype(BF16)
        p = ptf.T[0:m_rows].astype(BF16)
        xc = _dot(p, xb).astype(BF16)
        cw = _dot(p, c1) + _dot(p, c2) + _dot(p, c3)
        y = jnp.zeros((m_rows, D_MODEL), F32)
        for j in range(EXPERTS_PER_GROUP):
            hg = _dot(xc, wg_ref[j])
            hu = _dot(xc, wu_ref[j])
            h = (hg * _sigmoid(hg)) * hu * cw[:, j:j + 1]
            y = y + _dot(h.astype(BF16), wd_ref[j])
        yh = y.astype(BF16)
        yl = (y - yh.astype(F32)).astype(BF16)
        if m_pad > m_rows:
            pad = jnp.zeros((m_pad - m_rows, D_MODEL), BF16)
            yh = jnp.concatenate([yh, pad], axis=0)
            yl = jnp.concatenate([yl, pad], axis=0)
        o_ref[...] += _dot(pt, yh) + _dot(pt, yl)

    @pl.when(count <= MOE_CHUNK)
    def _():
        @pl.when(count > 0)
        def _():
            run_chunk(0, MOE_CHUNK)

    @pl.when((count > MOE_CHUNK) & (count <= MOE_CHUNK_MID))
    def _():
        run_chunk(0, MOE_CHUNK_MID)

    @pl.when(count > MOE_CHUNK_MID)
    def _():
        def big_chunk(k, carry):
            run_chunk(k * 2 * MOE_CHUNK, 2 * MOE_CHUNK)
            return carry
        lax.fori_loop(0, (count + 2 * MOE_CHUNK - 1) // (2 * MOE_CHUNK), big_chunk, 0)


def _moe_single_tile(x, rt, ltri, wg, wu, wd):
    n = x.shape[0]
    row = lambda c: pl.BlockSpec((n, c), lambda t, g: (0, 0))
    grp = lambda a, b: pl.BlockSpec((EXPERTS_PER_GROUP, a, b), lambda t, g: (g, 0, 0))
    return pl.pallas_call(
        functools.partial(_moe_kernel, tm=n, group=None, accumulate=False),
        grid=(1, N_GROUPS),
        in_specs=[row(D_MODEL), row(LANE), pl.BlockSpec(ltri.shape, lambda t, g: (0, 0)),
                  grp(D_MODEL, D_EXPERT), grp(D_MODEL, D_EXPERT), grp(D_EXPERT, D_MODEL)],
        out_specs=row(D_MODEL),
        out_shape=jax.ShapeDtypeStruct((n, D_MODEL), F32),
        compiler_params=_params(2),
        name="grouped_moe_single_tile",
    )(x, rt, ltri, wg, wu, wd)


def _moe(x, rt, ltri, wg, wu, wd, tm):
    n = x.shape[0]
    row = lambda c: pl.BlockSpec((tm, c), lambda t: (t, 0))
    acc = None
    for g in range(N_GROUPS):
        grp = lambda a, b: pl.BlockSpec((EXPERTS_PER_GROUP, a, b), lambda t, g=g: (g, 0, 0),
                                        pipeline_mode=pl.Buffered(1))
        first = acc is None
        acc = pl.pallas_call(
            functools.partial(_moe_kernel, tm=tm, group=g, accumulate=not first),
            grid=(n // tm,),
            in_specs=[row(D_MODEL), row(LANE)] + ([] if first else [row(D_MODEL)]) + [
                _resident(ltri.shape), grp(D_MODEL, D_EXPERT), grp(D_MODEL, D_EXPERT), grp(D_EXPERT, D_MODEL)],
            out_specs=row(D_MODEL),
            out_shape=jax.ShapeDtypeStruct((n, D_MODEL), F32),
            input_output_aliases={} if first else {2: 0},
            compiler_params=_params(1),
            name="grouped_moe",
        )(*((x, rt) + (() if first else (acc,)) + (ltri, wg, wu, wd)))
    return acc


def _ln2_kernel(x_ref, y_ref, g_ref, b_ref, o_ref):
    o_ref[...] = _layer_norm(ALPHA * x_ref[...] + y_ref[...], g_ref[...], b_ref[...])


def _ln2(x, y, g, b, tm):
    n = x.shape[0]
    row = pl.BlockSpec((tm, D_MODEL), lambda i: (i, 0))
    vec = _resident((1, D_MODEL))
    return pl.pallas_call(
        _ln2_kernel, grid=(n // tm,), in_specs=[row, row, vec, vec], out_specs=row,
        out_shape=jax.ShapeDtypeStruct((n, D_MODEL), F32),
        compiler_params=_params(1), name="moe_residual_ln",
    )(x, y, g, b)


PAGES_PER_STEP = 32
ROWS_PER_STEP = PAGES_PER_STEP * PAGE_SIZE
BLOCKS_PER_STEP = ROWS_PER_STEP // BLOCK
ATTN_BLOCKS_PER_STEP = LANE
ATTN_ROWS_PER_STEP = ATTN_BLOCKS_PER_STEP * BLOCK
HALF = KV_HEADS * HEAD_DIM * 2
ROW_SLABS = KV_SLOTS * KV_HEADS
WIN_SLABS = 2 * KV_HEADS


def _page_specs(layer, n_pages):
    def spec(j):
        return pl.BlockSpec((None, None, PAGE_SIZE * ROW_SLABS, HEAD_DIM),
                            lambda b, s, pt: (layer, pt[b * n_pages + s * PAGES_PER_STEP + j], 0, 0))
    return [spec(j) for j in range(PAGES_PER_STEP)]


def _slab(page, c):
    return page[pl.ds(c, PAGE_SIZE, stride=ROW_SLABS), :]


def _sample_compress_kernel(pt_ref, *refs):
    pages = refs[:PAGES_PER_STEP]
    pe_ref, w1_ref, w2_ref, o_ref, ksel_ref, vsel_ref, acc_ref = refs[PAGES_PER_STEP:]
    for j, page in enumerate(pages):
        rows = slice(j * PAGE_SIZE, (j + 1) * PAGE_SIZE)
        for g in range(KV_HEADS):
            ksel_ref[g, rows, :] = _slab(page, 2 * KV_HEADS + g).astype(BF16)
            vsel_ref[g, rows, :] = _slab(page, 3 * KV_HEADS + g).astype(BF16)
    m_rows = BLOCKS_PER_STEP * ROW_SLABS
    acc = jnp.zeros((m_rows, 2 * HEAD_DIM), F32)
    for lp in range(BLOCK // 2):
        parts = []
        for l in (2 * lp, 2 * lp + 1):
            tiles = [page[(h * BLOCK + l) * ROW_SLABS:(h * BLOCK + l + 1) * ROW_SLABS, :]
                     for page in pages for h in range(PAGE_SIZE // BLOCK)]
            pe = jnp.concatenate([pe_ref[l]] * BLOCKS_PER_STEP, axis=0)
            parts.append(jnp.concatenate(tiles, axis=0) + pe)
        lhs = jnp.concatenate(parts, axis=1).astype(BF16)
        acc = acc + _dot(lhs, w1_ref[lp])
    for s in range(2):
        acc_ref[s] = acc[:, s * HEAD_DIM:(s + 1) * HEAD_DIM]
    for s in range(2):
        for g in range(KV_HEADS):
            c = s * KV_HEADS + g
            h = _gelu(acc_ref[s, pl.ds(c, BLOCKS_PER_STEP, stride=ROW_SLABS), :]).astype(BF16)
            o_ref[c] = _dot(h, w2_ref[s]).astype(BF16)


def _sample_compress(page_table, cache, layer, pe_rows, w1_cat, w2, db, n_pages):
    n_steps = n_pages // PAGES_PER_STEP
    nblk = n_pages * PAGE_SIZE // BLOCK
    const = lambda shape: pl.BlockSpec(shape, lambda b, s, pt: (0,) * len(shape))
    sel_spec = pl.BlockSpec((None, KV_HEADS, ROWS_PER_STEP, HEAD_DIM), lambda b, s, pt: (b, 0, s, 0))
    sel_shape = jax.ShapeDtypeStruct((db, KV_HEADS, n_pages * PAGE_SIZE, HEAD_DIM), BF16)
    return pl.pallas_call(
        _sample_compress_kernel,
        grid_spec=pltpu.PrefetchScalarGridSpec(
            num_scalar_prefetch=1, grid=(db, n_steps),
            in_specs=_page_specs(layer, n_pages) + [const(pe_rows.shape), const(w1_cat.shape), const(w2.shape)],
            out_specs=[pl.BlockSpec((None, 4, BLOCKS_PER_STEP, HEAD_DIM), lambda b, s, pt: (b, 0, s, 0)),
                       sel_spec, sel_spec],
            scratch_shapes=[pltpu.VMEM((2, BLOCKS_PER_STEP * ROW_SLABS, HEAD_DIM), F32)]),
        out_shape=[jax.ShapeDtypeStruct((db, 4, nblk, HEAD_DIM), BF16), sel_shape, sel_shape],
        compiler_params=_params(2),
        name="sample_compress",
    )(page_table, *([cache] * PAGES_PER_STEP), pe_rows, w1_cat, w2)


def _compress_operands(pe, w1):
    pe_rows = jnp.zeros((BLOCK, ROW_SLABS, HEAD_DIM), F32)
    for s in range(2):
        for g in range(KV_HEADS):
            pe_rows = pe_rows.at[:, s * KV_HEADS + g, :].set(pe[s])
    w1_cat = w1.transpose(1, 2, 0, 3).reshape(BLOCK // 2, 2 * HEAD_DIM, 2 * HEAD_DIM).astype(BF16)
    return pe_rows, w1_cat


def _softmax_update(s, v, m, l, acc):
    m_new = jnp.maximum(m, jnp.max(s, axis=1, keepdims=True))
    a = jnp.exp2(m - m_new)
    p = jnp.exp2(s - m_new)
    return m_new, a * l + jnp.sum(p, axis=1, keepdims=True), a * acc + _dot(p.astype(BF16), v)


def _sample_attn_kernel(k_ref, v_ref, q_ref, gates_ref, kvc_ref, new_ref, nwin_ref, nwin4_ref, cwin_ref, e_ref,
                        o_ref, wout_ref, sel_ref, ocmp_ref, m_ref, l_ref, acc_ref, *, ds, past_len, n_steps):
    step = pl.program_id(1)
    rows = Q_PER_KV * ds
    nc = past_len // BLOCK
    ns = -(-(past_len + ds) // BLOCK)
    ns_pad = -(-ns // LANE) * LANE
    qi = lax.broadcasted_iota(I32, (rows, 1), 0) % ds
    qpos = past_len + qi

    def q_group(g):
        return jnp.concatenate([q_ref[:, (g * Q_PER_KV + r) * HEAD_DIM:(g * Q_PER_KV + r + 1) * HEAD_DIM]
                                for r in range(Q_PER_KV)], axis=0).astype(BF16)

    @pl.when(step == 0)
    def _():
        blk_c = lax.broadcasted_iota(I32, (rows, nc), 1)
        cmask = (blk_c + 1) * BLOCK - 1 <= qpos
        blk = lax.broadcasted_iota(I32, (ds, ns_pad), 1)
        qpos_q = past_len + lax.broadcasted_iota(I32, (ds, 1), 0)
        cur = qpos_q // BLOCK
        forced = (blk == 0) | (blk == cur) | (blk == cur - 1)
        valid = (blk <= cur) & (blk < ns)
        for g in range(KV_HEADS):
            qg = q_group(g)
            sc = jnp.where(cmask, lax.dot_general(qg, kvc_ref[g], _NT, preferred_element_type=F32), NEG)
            m = jnp.max(sc, axis=1, keepdims=True)
            e = jnp.where(cmask, jnp.exp2(sc - m), 0.0)
            p = e / jnp.maximum(jnp.sum(e, axis=1, keepdims=True), 1e-30)
            ocmp_ref[g] = _dot(p.astype(BF16), kvc_ref[KV_HEADS + g])
            imp = p[0:ds]
            for r in range(1, Q_PER_KV):
                imp = imp + p[r * ds:(r + 1) * ds]
            imp = jnp.concatenate([imp, jnp.zeros((ds, ns_pad - nc), F32)], axis=1)
            score = jnp.where(valid, jnp.where(forced, FORCED_SCORE, imp), -1.0)
            rank = jnp.zeros((ds, ns_pad), I32)
            for j in range(ns):
                cj = score[:, j:j + 1]
                rank = rank + ((cj > score) | ((cj == score) & (blk > j))).astype(I32)
            sel = ((rank < min(N_SELECT, ns)) & valid).astype(F32)
            sel = jnp.concatenate([sel] * Q_PER_KV, axis=0)
            for t in range(n_steps):
                sel_ref[g, t] = sel[:, t * ATTN_BLOCKS_PER_STEP:(t + 1) * ATTN_BLOCKS_PER_STEP].astype(BF16)
            sel_ref[g, n_steps] = jnp.broadcast_to(sel[:, nc:nc + 1], (rows, LANE)).astype(BF16)
            m_ref[g] = jnp.full((rows, 1), NEG, F32)
            l_ref[g] = jnp.zeros((rows, 1), F32)
            acc_ref[g] = jnp.zeros((rows, HEAD_DIM), F32)

    for g in range(KV_HEADS):
        qg = q_group(g)
        hit = _dot(sel_ref[g, step], e_ref[...])
        s = lax.dot_general(qg, k_ref[g], _NT, preferred_element_type=F32)
        s = jnp.where(hit > 0.5, s, NEG)
        m, l, acc = _softmax_update(s, v_ref[g], m_ref[g], l_ref[g], acc_ref[g])
        m_ref[g] = m
        l_ref[g] = l
        acc_ref[g] = acc

    @pl.when(step == n_steps - 1)
    def _():
        pad = jnp.zeros((2 * SUBLANE - ds, HEAD_DIM), F32)
        col = lax.broadcasted_iota(I32, (rows, 2 * SUBLANE), 1)
        new_ok = (col <= qi) & (col < ds)
        wpos = lax.broadcasted_iota(I32, (rows, WINDOW), 1)
        for g in range(KV_HEADS):
            qg = q_group(g)
            kn = jnp.concatenate([new_ref[:, (2 * KV_HEADS + g) * HEAD_DIM:(2 * KV_HEADS + g + 1) * HEAD_DIM], pad], axis=0).astype(BF16)
            vn = jnp.concatenate([new_ref[:, (3 * KV_HEADS + g) * HEAD_DIM:(3 * KV_HEADS + g + 1) * HEAD_DIM], pad], axis=0).astype(BF16)
            s = lax.dot_general(qg, kn, _NT, preferred_element_type=F32)
            picked = sel_ref[g, n_steps][:, 0:2 * SUBLANE].astype(F32) > 0.5
            s = jnp.where(new_ok & picked, s, NEG)
            m, l, acc = _softmax_update(s, vn, m_ref[g], l_ref[g], acc_ref[g])
            o_sel = acc / l
            kw = cwin_ref[pl.ds(g, WINDOW, stride=WIN_SLABS), :].astype(BF16)
            vw = cwin_ref[pl.ds(KV_HEADS + g, WINDOW, stride=WIN_SLABS), :].astype(BF16)
            s = lax.dot_general(qg, kw, _NT, preferred_element_type=F32)
            s = jnp.where(wpos > qi, s, NEG)
            m, l, acc = _softmax_update(s, vw, jnp.full((rows, 1), NEG, F32), jnp.zeros((rows, 1), F32),
                                        jnp.zeros((rows, HEAD_DIM), F32))
            kn = jnp.concatenate([nwin_ref[:, g * HEAD_DIM:(g + 1) * HEAD_DIM], pad], axis=0).astype(BF16)
            vn = jnp.concatenate([nwin_ref[:, (KV_HEADS + g) * HEAD_DIM:(KV_HEADS + g + 1) * HEAD_DIM], pad], axis=0).astype(BF16)
            s = lax.dot_general(qg, kn, _NT, preferred_element_type=F32)
            s = jnp.where(new_ok, s, NEG)
            m, l, acc = _softmax_update(s, vn, m, l, acc)
            o_win = acc / l
            o_cmp = ocmp_ref[g]
            for r in range(Q_PER_KV):
                rs = slice(r * ds, (r + 1) * ds)
                c0 = g * LANE
                g_cmp = gates_ref[:, c0 + r:c0 + r + 1]
                g_sel = gates_ref[:, c0 + Q_PER_KV + r:c0 + Q_PER_KV + r + 1]
                g_win = gates_ref[:, c0 + 2 * Q_PER_KV + r:c0 + 2 * Q_PER_KV + r + 1]
                h = g * Q_PER_KV + r
                o_ref[:, h * HEAD_DIM:(h + 1) * HEAD_DIM] = g_cmp * o_cmp[rs] + g_sel * o_sel[rs] + g_win * o_win[rs]
        wout_ref[0:(WINDOW - ds) * WIN_SLABS, :] = cwin_ref[ds * WIN_SLABS:WINDOW * WIN_SLABS, :]
        wout_ref[(WINDOW - ds) * WIN_SLABS:WINDOW * WIN_SLABS, :] = nwin4_ref[...]


def _sample_attention(ksel, vsel, layer, q, gates, kvc, kv4, kwin, kwin4, cache_win, expand, db, ds):
    past_len = ksel.shape[2]
    n_steps = past_len // ATTN_ROWS_PER_STEP
    assert ds == SUBLANE and past_len % ATTN_ROWS_PER_STEP == 0 and past_len >= WINDOW
    assert cache_win.shape[2] == WINDOW * WIN_SLABS
    rows = Q_PER_KV * ds
    per_b = lambda c: pl.BlockSpec((ds, c), lambda b, s: (b, 0))
    sel_spec = pl.BlockSpec((None, KV_HEADS, ATTN_ROWS_PER_STEP, HEAD_DIM), lambda b, s: (b, 0, s, 0))
    return pl.pallas_call(
        functools.partial(_sample_attn_kernel, ds=ds, past_len=past_len, n_steps=n_steps),
        grid=(db, n_steps),
        in_specs=[sel_spec, sel_spec,
                  per_b(D_ATTN), per_b(KV_HEADS * LANE),
                  pl.BlockSpec((None, 4, past_len // BLOCK, HEAD_DIM), lambda b, s: (b, 0, 0, 0)),
                  per_b(2 * HALF), per_b(HALF),
                  pl.BlockSpec((ds * WIN_SLABS, HEAD_DIM), lambda b, s: (b, 0)),
                  pl.BlockSpec((None, None, WINDOW * WIN_SLABS, HEAD_DIM), lambda b, s: (layer, b, 0, 0)),
                  pl.BlockSpec(expand.shape, lambda b, s: (0, 0))],
        out_specs=[per_b(D_ATTN),
                   pl.BlockSpec((None, WINDOW * WIN_SLABS, HEAD_DIM), lambda b, s: (b, 0, 0))],
        scratch_shapes=[pltpu.VMEM((KV_HEADS, n_steps + 1, rows, LANE), BF16),
                        pltpu.VMEM((KV_HEADS, rows, HEAD_DIM), F32),
                        pltpu.VMEM((KV_HEADS, rows, 1), F32),
                        pltpu.VMEM((KV_HEADS, rows, 1), F32),
                        pltpu.VMEM((KV_HEADS, rows, HEAD_DIM), F32)],
        out_shape=[jax.ShapeDtypeStruct((db * ds, D_ATTN), F32),
                   jax.ShapeDtypeStruct((db, WINDOW * WIN_SLABS, HEAD_DIM), F32)],
        compiler_params=_params(2),
        name="sample_attention",
    )(ksel, vsel, q, gates, kvc, kv4, kwin, kwin4, cache_win, expand)


def _gate_columns(w_gates):
    d = w_gates.shape[0]
    w = w_gates.reshape(d, 3, KV_HEADS, Q_PER_KV).transpose(0, 2, 1, 3).reshape(d, KV_HEADS, 3 * Q_PER_KV)
    return jnp.pad(w, ((0, 0), (0, 0), (0, LANE - 3 * Q_PER_KV))).reshape(d, KV_HEADS * LANE)


def kernel(x_prompt, x_sample, cache_kv, cache_win, state_conv, state_h, page_table, w_in, cmp_pe, cmp_w1, cmp_w2, conv_w, conv_b, lru_w_a, lru_b_a, lru_w_i, lru_b_i, lru_lam, group_norm, w_out, ln_g, ln_b, router_w, router_b, exp_w_gate, exp_w_up, exp_w_down):
    b, t, _ = x_prompt.shape
    db, ds, _ = x_sample.shape
    past_len = page_table.shape[1] * PAGE_SIZE
    n_p, n_s = b * t, db * ds
    tm_p, tm_moe, tq, tk, tt = 256, 512, 512, 1024, 256
    n_s_pad = -(-n_s // LANE) * LANE

    xp = x_prompt.reshape(n_p, D_MODEL)
    xs = x_sample.reshape(n_s, D_MODEL)
    expand = (jnp.arange(t)[None, :] // BLOCK == jnp.arange(t // BLOCK)[:, None]).astype(BF16)
    ltri = lambda m: (jnp.arange(m)[None, :] < jnp.arange(m)[:, None]).astype(BF16)
    ltri_p, ltri_s = ltri(tm_moe), ltri(n_s_pad)
    rw = jnp.pad(router_w, ((0, 0), (0, LANE - N_EXPERTS))).astype(BF16)
    rb = jnp.pad(router_b, (0, LANE - N_EXPERTS)).reshape(1, LANE)
    conv0_p = jnp.zeros((b, SUBLANE, D_RNN), F32)
    h0_p = jnp.zeros((b, SUBLANE, D_RNN), F32)
    pad_rows = SUBLANE - (CONV_WIDTH - 1)
    n_pages = page_table.shape[1]
    pt_flat = page_table.reshape(-1)
    cache = cache_kv.reshape(DEPTH, cache_kv.shape[1], PAGE_SIZE * ROW_SLABS, HEAD_DIM)
    cwin = cache_win.reshape(DEPTH, db, cache_win.shape[2] * WIN_SLABS, HEAD_DIM)
    expand_s = (jnp.arange(ATTN_ROWS_PER_STEP)[None, :] // BLOCK
                == jnp.arange(ATTN_BLOCKS_PER_STEP)[:, None]).astype(BF16)

    outs = [[] for _ in range(8)]
    for l in range(DEPTH):
        c_gate = 1024 + 6 * KV_HEADS * HEAD_DIM
        w_main = jnp.concatenate([w_in[l, :, :c_gate], w_in[l, :, c_gate + 3 * N_HEADS:]], axis=1).astype(BF16)
        w_gate = _gate_columns(w_in[l, :, c_gate:c_gate + 3 * N_HEADS]).astype(BF16)
        w1 = cmp_w1[l].astype(BF16)
        w2 = cmp_w2[l].astype(BF16)
        w_ai = jnp.concatenate([lru_w_a[l], lru_w_i[l]], axis=-1).astype(BF16)
        vec = lambda v: v.reshape(1, -1)
        lru = (conv_w[l], vec(conv_b[l]), w_ai, vec(lru_b_a[l]), vec(lru_b_i[l]), vec(lru_lam[l]))
        wo = w_out[l].astype(BF16)
        wg, wu, wd = exp_w_gate[l].astype(BF16), exp_w_up[l].astype(BF16), exp_w_down[l].astype(BF16)
        gn = vec(group_norm[l])

        q, kv4, kwin, kvb, gates, u, gg = _in_proj(xp, w_main, w_gate, tm_p)
        kvc = _compress(kv4, cmp_pe[l], w1, w2, b, t)
        attn = _prompt_attention(q, gates, kvc, kvb, expand, b, t, tq, tk)
        rnn, hl = _rglru(u, gg, conv0_p, h0_p, *lru, b, t, tt)
        x1, x1b, rt = _mixer(attn, rnn, xp, gn, wo, vec(ln_g[l, 0]), vec(ln_b[l, 0]), rw, rb, tm_p)
        y = _moe(x1b, rt, ltri_p, wg, wu, wd, tm_moe)
        xp = _ln2(x1, y, vec(ln_g[l, 1]), vec(ln_b[l, 1]), tm_p)
        outs[0].append(kv4.reshape(b, t, KV_SLOTS, KV_HEADS, HEAD_DIM))
        outs[2].append(kwin.reshape(b, t, 2, KV_HEADS, HEAD_DIM)[:, -min(WINDOW, t):])
        outs[4].append(u.reshape(b, t, D_RNN)[:, -(CONV_WIDTH - 1):])
        outs[6].append(hl.reshape(b, D_RNN))

        q, kv4, kwin, kvb, gates, u, gg = _in_proj(xs, w_main, w_gate, n_s)
        kvc, ksel, vsel = _sample_compress(pt_flat, cache, l, *_compress_operands(cmp_pe[l], cmp_w1[l]), w2,
                                           db, n_pages)
        attn, wnew = _sample_attention(ksel, vsel, l, q.astype(F32), gates, kvc, kv4, kwin,
                                       kwin.reshape(n_s * WIN_SLABS, HEAD_DIM), cwin, expand_s, db, ds)
        conv0_s = jnp.pad(state_conv[l], ((0, 0), (pad_rows, 0), (0, 0)))
        h0_s = jnp.pad(state_h[l][:, None, :], ((0, 0), (0, SUBLANE - 1), (0, 0)))
        rnn, hl = _rglru(u, gg, conv0_s, h0_s, *lru, db, ds, ds)
        x1, x1b, rt = _mixer(attn, rnn, xs, gn, wo, vec(ln_g[l, 0]), vec(ln_b[l, 0]),
                        rw, rb, n_s)
        x1p = jnp.pad(x1b, ((0, n_s_pad - n_s), (0, 0)))
        rtp = jnp.pad(rt, ((0, n_s_pad - n_s), (0, 0)), constant_values=-1.0)
        y = _moe_single_tile(x1p, rtp, ltri_s, wg, wu, wd)[:n_s]
        xs = _ln2(x1, y, vec(ln_g[l, 1]), vec(ln_b[l, 1]), n_s)
        outs[1].append(kv4.reshape(db, ds, KV_SLOTS, KV_HEADS, HEAD_DIM))
        outs[3].append(wnew.reshape(db, WINDOW, 2, KV_HEADS, HEAD_DIM))
        xpc = jnp.concatenate([state_conv[l], u.reshape(db, ds, D_RNN)], axis=1)
        outs[5].append(xpc[:, -(CONV_WIDTH - 1):])
        outs[7].append(hl.reshape(db, D_RNN))

    st = [jnp.stack(o) for o in outs]
    return (xp.reshape(b, t, D_MODEL), xs.reshape(db, ds, D_MODEL),
            st[0], st[1], st[2], st[3], st[4], st[5], st[6], st[7])
```

```python
import functools

import jax
import jax.numpy as jnp
from jax import lax
from jax.experimental import pallas as pl
from jax.experimental.pallas import tpu as pltpu

F32 = jnp.float32
BF16 = jnp.bfloat16
I32 = jnp.int32

D_MODEL = 2048
DEPTH = 4
PAGE_SIZE = 128
D_ATTN = D_MODEL // 2
D_RNN = D_MODEL - D_ATTN
N_HEADS = 8
HEAD_DIM = D_ATTN // N_HEADS
KV_HEADS = 2
Q_PER_KV = N_HEADS // KV_HEADS
BLOCK = 64
N_SELECT = 16
WINDOW = 512
FORCED_SCORE = 1e4
KV_SLOTS = 4
CONV_WIDTH = 4
LRU_BLOCKS = 8
LRU_BW = D_RNN // LRU_BLOCKS
LRU_C = 8.0
N_EXPERTS = 16
N_GROUPS = 4
EXPERTS_PER_GROUP = N_EXPERTS // N_GROUPS
D_EXPERT = 512
ALPHA = (2 * DEPTH) ** 0.25
LN_EPS = 1e-5
SCALE = HEAD_DIM ** -0.5
Q_SCALE = SCALE * 1.4426950408889634
NEG = -1e30

LANE = 128
SUBLANE = 8
MOE_CHUNK = 128
VMEM_LIMIT = 56 * 1024 * 1024

_NT = (((1,), (1,)), ((), ()))
_TN = (((0,), (0,)), ((), ()))


def _params(n_axes):
    return pltpu.CompilerParams(dimension_semantics=("arbitrary",) * n_axes,
                                vmem_limit_bytes=VMEM_LIMIT)


def _sigmoid(x):
    return 1.0 / (1.0 + jnp.exp(-x))


def _gelu(x):
    return 0.5 * x * (1.0 + jnp.tanh(0.7978845608028654 * (x + 0.044715 * (x * x * x))))


def _dot(a, b):
    return jnp.dot(a, b, preferred_element_type=F32)


def _resident(shape):
    nd = len(shape)
    return pl.BlockSpec(shape, lambda *_: (0,) * nd, pipeline_mode=pl.Buffered(1))


def _in_proj_kernel(x_ref, w_ref, wg_ref, q_ref, kv4_ref, kwin_ref, kvb_ref, gates_ref, u_ref, gg_ref):
    xb = x_ref[...].astype(BF16)

    def mm(lo, hi):
        return _dot(xb, w_ref[:, lo:hi])

    q_ref[...] = (mm(0, 1024) * Q_SCALE).astype(BF16)
    kv = mm(1024, 2048)
    kv4_ref[...] = kv
    kw = mm(2048, 2560)
    kwin_ref[...] = kw
    kvb_ref[:, 0:512] = kv[:, 512:1024].astype(BF16)
    kvb_ref[:, 512:1024] = kw.astype(BF16)
    gates_ref[...] = _sigmoid(_dot(xb, wg_ref[...]))
    u_ref[...] = mm(2560, 3584)
    gg_ref[...] = mm(3584, 4608)


def _in_proj(x, w_main, w_gate, tm):
    n = x.shape[0]
    row = lambda c: pl.BlockSpec((tm, c), lambda i: (i, 0))
    return pl.pallas_call(
        _in_proj_kernel,
        grid=(n // tm,),
        in_specs=[row(D_MODEL), _resident(w_main.shape), _resident(w_gate.shape)],
        out_specs=[row(1024), row(1024), row(512), row(1024), row(256), row(1024), row(1024)],
        out_shape=[jax.ShapeDtypeStruct((n, 1024), BF16),
                   jax.ShapeDtypeStruct((n, 1024), F32),
                   jax.ShapeDtypeStruct((n, 512), F32),
                   jax.ShapeDtypeStruct((n, 1024), BF16),
                   jax.ShapeDtypeStruct((n, 256), F32),
                   jax.ShapeDtypeStruct((n, 1024), F32),
                   jax.ShapeDtypeStruct((n, 1024), F32)],
        compiler_params=_params(1),
        name="in_proj",
    )(x, w_main, w_gate)


def _compress_kernel(x0_ref, x1_ref, x2_ref, x3_ref, pe_ref, w1_ref, w2_ref, o_ref, *, nblk):
    x_refs = (x0_ref, x1_ref, x2_ref, x3_ref)
    for s in range(2):
        acc = [jnp.zeros((nblk, HEAD_DIM), F32) for _ in range(KV_HEADS)]
        for l in range(BLOCK):
            pe = pe_ref[s, l:l + 1, :]
            w = w1_ref[s, l]
            for g in range(KV_HEADS):
                rows = x_refs[s * KV_HEADS + g][pl.ds(l, nblk, stride=BLOCK), :]
                acc[g] = acc[g] + _dot((rows + pe).astype(BF16), w)
        for g in range(KV_HEADS):
            h = _gelu(acc[g]).astype(BF16)
            o_ref[s * KV_HEADS + g] = _dot(h, w2_ref[s]).astype(BF16)


def _compress(kv4, pe, w1, w2, batch, t_len):
    nblk = t_len // BLOCK
    col = lambda c: pl.BlockSpec((t_len, HEAD_DIM), lambda b: (b, c))
    return pl.pallas_call(
        functools.partial(_compress_kernel, nblk=nblk),
        grid=(batch,),
        in_specs=[col(0), col(1), col(2), col(3),
                  _resident(pe.shape), _resident(w1.shape), _resident(w2.shape)],
        out_specs=pl.BlockSpec((None, 4, nblk, HEAD_DIM), lambda b: (b, 0, 0, 0)),
        out_shape=jax.ShapeDtypeStruct((batch, 4, nblk, HEAD_DIM), BF16),
        compiler_params=_params(1),
        name="compress",
    )(kv4, kv4, kv4, kv4, pe, w1, w2)


def _attn_kernel(q_ref, gates_ref, kc_ref, vc_ref, ks_ref, vs_ref, kw_ref, vw_ref, e_ref, o_ref,
                 *, tq, tk, t_len):
    i = pl.program_id(2)
    s0 = i * tq
    nblk = t_len // BLOCK
    qpos = s0 + lax.broadcasted_iota(I32, (tq, 1), 0)
    qh = [q_ref[:, r * HEAD_DIM:(r + 1) * HEAD_DIM] for r in range(Q_PER_KV)]

    blk = lax.broadcasted_iota(I32, (nblk, tq), 0)
    qpos_l = s0 + lax.broadcasted_iota(I32, (1, tq), 1)
    cmask = (blk + 1) * BLOCK - 1 <= qpos_l
    kc = kc_ref[...]
    vc = vc_ref[...]
    imp = jnp.zeros((nblk, tq), F32)
    o_cmp = []
    for r in range(Q_PER_KV):
        sc = jnp.where(cmask, lax.dot_general(kc, qh[r], _NT, preferred_element_type=F32), NEG)
        m = jnp.max(sc, axis=0, keepdims=True)
        e = jnp.where(cmask, jnp.exp2(sc - m), 0.0)
        p = e / jnp.maximum(jnp.sum(e, axis=0, keepdims=True), 1e-30)
        imp = imp + p
        o_cmp.append(lax.dot_general(p.astype(BF16), vc, _TN, preferred_element_type=F32))

    cur = jnp.right_shift(qpos_l, 6)
    forced = (blk == 0) | (blk == cur) | (blk == cur - 1)
    valid = blk <= cur
    score = jnp.where(valid, jnp.where(forced, FORCED_SCORE, imp), -1.0)
    rank = jnp.zeros((nblk, tq), I32)
    for j in range(nblk):
        cj = score[j:j + 1, :]
        beats = (cj > score) | ((cj == score) & (blk > j))
        rank = rank + beats.astype(I32)
    sel = ((rank < min(N_SELECT, nblk)) & valid).astype(BF16)

    def sel_body(kt, carry):
        off = pl.multiple_of(kt * tk, tk)
        k = ks_ref[pl.ds(off, tk), :]
        v = vs_ref[pl.ds(off, tk), :]
        hit = lax.dot_general(sel, e_ref[:, pl.ds(off, tk)], _TN, preferred_element_type=F32)
        kpos = off + lax.broadcasted_iota(I32, (tq, tk), 1)
        bias = jnp.where((hit > 0.5) & (kpos <= qpos), 0.0, NEG)
        out = []
        for r in range(Q_PER_KV):
            m, l, acc = carry[r]
            s = lax.dot_general(qh[r], k, _NT, preferred_element_type=F32) + bias
            m_new = jnp.maximum(m, jnp.max(s, axis=1, keepdims=True))
            a = jnp.exp2(m - m_new)
            p = jnp.exp2(s - m_new)
            l = a * l + jnp.sum(p, axis=1, keepdims=True)
            acc = a * acc + _dot(p.astype(BF16), v)
            out.append((m_new, l, acc))
        return tuple(out)

    init = tuple((jnp.full((tq, 1), NEG, F32), jnp.zeros((tq, 1), F32), jnp.zeros((tq, HEAD_DIM), F32))
                 for _ in range(Q_PER_KV))
    n_t = (s0 + tq + tk - 1) // tk
    fin = lax.fori_loop(0, n_t, sel_body, init)
    o_sel = [acc / l for (_, l, acc) in fin]

    wlen = WINDOW + tq
    start = pl.multiple_of(jnp.maximum(s0 - WINDOW, 0), tq)
    kw = kw_ref[pl.ds(start, wlen), :]
    vw = vw_ref[pl.ds(start, wlen), :]
    kposw = start + lax.broadcasted_iota(I32, (tq, wlen), 1)
    biasw = jnp.where((kposw <= qpos) & (kposw > qpos - WINDOW), 0.0, NEG)

    for r in range(Q_PER_KV):
        s = lax.dot_general(qh[r], kw, _NT, preferred_element_type=F32) + biasw
        p = jnp.exp2(s - jnp.max(s, axis=1, keepdims=True))
        o_win = _dot(p.astype(BF16), vw) / jnp.sum(p, axis=1, keepdims=True)
        g_cmp = gates_ref[:, r:r + 1]
        g_sel = gates_ref[:, Q_PER_KV + r:Q_PER_KV + r + 1]
        g_win = gates_ref[:, 2 * Q_PER_KV + r:2 * Q_PER_KV + r + 1]
        o_ref[:, r * HEAD_DIM:(r + 1) * HEAD_DIM] = g_cmp * o_cmp[r] + g_sel * o_sel[r] + g_win * o_win


def _prompt_attention(q, gates, kvc, kvb, expand, batch, t_len, tq, tk):
    nq = t_len // tq
    gw = Q_PER_KV * HEAD_DIM
    col = lambda c: pl.BlockSpec((t_len, HEAD_DIM), lambda b, g, i: (b, c + g))
    cmp_blk = lambda c: pl.BlockSpec((None, None, t_len // BLOCK, HEAD_DIM), lambda b, g, i: (b, c + g, 0, 0))
    return pl.pallas_call(
        functools.partial(_attn_kernel, tq=tq, tk=tk, t_len=t_len),
        grid=(batch, KV_HEADS, nq),
        in_specs=[pl.BlockSpec((tq, gw), lambda b, g, i: (b * nq + i, g)),
                  pl.BlockSpec((tq, LANE), lambda b, g, i: (b * nq + i, g)),
                  cmp_blk(0), cmp_blk(2),
                  col(0), col(2), col(4), col(6),
                  pl.BlockSpec(expand.shape, lambda b, g, i: (0, 0))],
        out_specs=pl.BlockSpec((tq, gw), lambda b, g, i: (b * nq + i, g)),
        out_shape=jax.ShapeDtypeStruct((batch * t_len, D_ATTN), F32),
        compiler_params=_params(3),
        name="prompt_attention",
    )(q, gates, kvc, kvc, kvb, kvb, kvb, kvb, expand)


def _rglru_kernel(u_ref, gg_ref, c0_ref, h0_ref, cw_ref, cb_ref, wai_ref, ba_ref, bi_ref, lam_ref,
                  y_ref, hl_ref, ext_ref, a_ref, b_ref, tail_ref, h_ref, *, tt):
    t = pl.program_id(1)

    @pl.when(t == 0)
    def _():
        tail_ref[...] = c0_ref[...]
        h_ref[...] = h0_ref[...]

    u = u_ref[...]
    ext_ref[0:SUBLANE, :] = tail_ref[...]
    ext_ref[SUBLANE:SUBLANE + tt, :] = u
    base = SUBLANE - (CONV_WIDTH - 1)
    xc = cb_ref[...] + ext_ref[base:base + tt, :] * cw_ref[0:1, :]
    for j in range(1, CONV_WIDTH - 1):
        xc = xc + ext_ref[base + j:base + j + tt, :] * cw_ref[j:j + 1, :]
    xc = xc + u * cw_ref[CONV_WIDTH - 1:CONV_WIDTH, :]
    tail_ref[...] = ext_ref[tt:tt + SUBLANE, :]

    xcb = xc.astype(BF16)
    for n in range(LRU_BLOCKS):
        c = slice(n * LRU_BW, (n + 1) * LRU_BW)
        ai = _dot(xcb[:, c], wai_ref[n])
        r = _sigmoid(ai[:, :LRU_BW] + ba_ref[:, c])
        ig = _sigmoid(ai[:, LRU_BW:] + bi_ref[:, c])
        nl = -lam_ref[:, c]
        softplus = jnp.maximum(nl, 0.0) + jnp.log1p(jnp.exp(-jnp.abs(nl)))
        log_a = (-LRU_C * softplus) * r
        a = jnp.exp(log_a)
        gain = jnp.sqrt(-jnp.tanh(log_a) * (a * a + 1.0))
        a_ref[:, c] = a
        b_ref[:, c] = gain * (ig * xc[:, c])

    def group(gi, h):
        off = pl.multiple_of(gi * SUBLANE, SUBLANE)
        a8 = a_ref[pl.ds(off, SUBLANE), :]
        b8 = b_ref[pl.ds(off, SUBLANE), :]
        rows = []
        for r in range(SUBLANE):
            h = a8[r:r + 1, :] * h + b8[r:r + 1, :]
            rows.append(h)
        a_ref[pl.ds(off, SUBLANE), :] = jnp.concatenate(rows, axis=0)
        return h

    h = lax.fori_loop(0, tt // SUBLANE, group, h_ref[0:1, :])
    h_ref[0:1, :] = h
    hl_ref[...] = h
    y_ref[...] = a_ref[...] * _gelu(gg_ref[...])


def _rglru(u, gg, conv0, h0, conv_w, conv_b, w_ai, b_a, b_i, lam, batch, t_len, tt):
    nt = t_len // tt
    c = D_RNN
    tile = pl.BlockSpec((tt, c), lambda b, t: (b * nt + t, 0))
    vec = _resident((1, c))
    return pl.pallas_call(
        functools.partial(_rglru_kernel, tt=tt),
        grid=(batch, nt),
        in_specs=[tile, tile,
                  pl.BlockSpec((None, SUBLANE, c), lambda b, t: (b, 0, 0)),
                  pl.BlockSpec((None, SUBLANE, c), lambda b, t: (b, 0, 0)),
                  _resident(conv_w.shape), vec, _resident(w_ai.shape), vec, vec, vec],
        out_specs=[tile, pl.BlockSpec((None, 1, c), lambda b, t: (b, 0, 0))],
        out_shape=[jax.ShapeDtypeStruct((batch * t_len, c), F32),
                   jax.ShapeDtypeStruct((batch, 1, c), F32)],
        scratch_shapes=[pltpu.VMEM((tt + SUBLANE, c), F32), pltpu.VMEM((tt, c), F32),
                        pltpu.VMEM((tt, c), F32), pltpu.VMEM((SUBLANE, c), F32),
                        pltpu.VMEM((SUBLANE, c), F32)],
        compiler_params=_params(2),
        name="rglru",
    )(u, gg, conv0, h0, conv_w, conv_b, w_ai, b_a, b_i, lam)


def _layer_norm(z, g, b):
    zc = z - jnp.mean(z, axis=1, keepdims=True)
    var = jnp.mean(zc * zc, axis=1, keepdims=True)
    return zc * lax.rsqrt(var + LN_EPS) * g + b


def _mixer_kernel(attn_ref, rnn_ref, x_ref, gn_ref, wo_ref, lng_ref, lnb_ref, rw_ref, rb_ref,
                  x1_ref, x1b_ref, rt_ref):
    def rms(v, g):
        return (v * lax.rsqrt(jnp.mean(v * v, axis=1, keepdims=True) + LN_EPS) * g).astype(BF16)

    an = rms(attn_ref[...], gn_ref[:, 0:D_ATTN])
    rn = rms(rnn_ref[...], gn_ref[:, D_ATTN:D_MODEL])
    y = _dot(an, wo_ref[0:D_ATTN, :]) + _dot(rn, wo_ref[D_ATTN:D_MODEL, :])
    x1 = _layer_norm(ALPHA * x_ref[...] + y, lng_ref[...], lnb_ref[...])
    x1_ref[...] = x1
    x1b = x1.astype(BF16)
    x1b_ref[...] = x1b

    s = _sigmoid(_dot(x1b, rw_ref[...]))
    biased = s + rb_ref[...]
    sc = [s[:, e:e + 1] for e in range(N_EXPERTS)]
    bc = [biased[:, e:e + 1] for e in range(N_EXPERTS)]
    best = grp = None
    for g in range(N_GROUPS):
        v = bc[g * EXPERTS_PER_GROUP:(g + 1) * EXPERTS_PER_GROUP]
        top2 = None
        for i in range(EXPERTS_PER_GROUP):
            for j in range(i + 1, EXPERTS_PER_GROUP):
                pair = v[i] + v[j]
                top2 = pair if top2 is None else jnp.maximum(top2, pair)
        if g == 0:
            best, grp = top2, jnp.zeros_like(top2, dtype=I32)
        else:
            better = top2 > best
            grp = jnp.where(better, g, grp)
            best = jnp.where(better, top2, best)

    def in_group(cols, j):
        out = cols[j]
        for g in range(1, N_GROUPS):
            out = jnp.where(grp == g, cols[g * EXPERTS_PER_GROUP + j], out)
        return out

    bv = [in_group(bc, j) for j in range(EXPERTS_PER_GROUP)]
    sv = [in_group(sc, j) for j in range(EXPERTS_PER_GROUP)]

    def argmax_first(vals, skip=None):
        bestv = idx = None
        for j, v in enumerate(vals):
            vj = v if skip is None else jnp.where(skip == j, -jnp.inf, v)
            if j == 0:
                bestv, idx = vj, jnp.zeros_like(grp)
            else:
                better = vj > bestv
                idx = jnp.where(better, j, idx)
                bestv = jnp.where(better, vj, bestv)
        return idx

    i1 = argmax_first(bv)
    i2 = argmax_first(bv, skip=i1)

    def pick(vals, idx):
        out = vals[0]
        for j in range(1, len(vals)):
            out = jnp.where(idx == j, vals[j], out)
        return out

    s1 = pick(sv, i1)
    s2 = pick(sv, i2)
    den = s1 + s2
    w1 = s1 / den
    w2 = s2 / den
    lane = lax.broadcasted_iota(I32, rt_ref.shape, 1)
    rt = jnp.where(lane == EXPERTS_PER_GROUP, grp.astype(F32), 0.0)
    rt = rt + jnp.where(lane == i1, w1, 0.0) + jnp.where(lane == i2, w2, 0.0)
    rt_ref[...] = rt


def _mixer(attn, rnn, x, gn, w_out, layer, ln_g, ln_b, rw, rb, tm):
    n = x.shape[0]
    row = lambda c: pl.BlockSpec((tm, c), lambda i: (i, 0))
    vec = _resident((1, D_MODEL))
    w_spec = pl.BlockSpec((None, D_MODEL, D_MODEL), lambda i: (layer, 0, 0), pipeline_mode=pl.Buffered(1))
    return pl.pallas_call(
        _mixer_kernel,
        grid=(n // tm,),
        in_specs=[row(D_ATTN), row(D_RNN), row(D_MODEL), vec, w_spec, vec, vec,
                  _resident(rw.shape), _resident(rb.shape)],
        out_specs=[row(D_MODEL), row(D_MODEL), row(LANE)],
        out_shape=[jax.ShapeDtypeStruct((n, D_MODEL), F32),
                   jax.ShapeDtypeStruct((n, D_MODEL), BF16),
                   jax.ShapeDtypeStruct((n, LANE), F32)],
        compiler_params=_params(1),
        name="mixer_out",
    )(attn, rnn, x, gn, w_out, ln_g, ln_b, rw, rb)


def _split3(v):
    a = v.astype(BF16)
    r = v - a.astype(F32)
    b = r.astype(BF16)
    c = (r - b.astype(F32)).astype(BF16)
    return a, b, c


def _moe_kernel(*refs, tm, group, accumulate):
    if accumulate:
        x_ref, rt_ref, acc_ref, ltri_ref, wg_ref, wu_ref, wd_ref, o_ref = refs
    else:
        x_ref, rt_ref, ltri_ref, wg_ref, wu_ref, wd_ref, o_ref = refs
    g = pl.program_id(1) if group is None else group
    rt = rt_ref[...]
    member = rt[:, EXPERTS_PER_GROUP:EXPERTS_PER_GROUP + 1] == jnp.asarray(g, F32)
    mf = member.astype(F32)
    pos = _dot(ltri_ref[...], jnp.broadcast_to(mf, (tm, LANE)).astype(BF16))[:, 0:1].astype(I32)
    count = jnp.sum(mf).astype(I32)
    xb = x_ref[...]
    c1, c2, c3 = _split3(rt)
    if accumulate:
        o_ref[...] = acc_ref[...]
    elif group is None:
        @pl.when(g == 0)
        def _():
            o_ref[...] = jnp.zeros_like(o_ref)
    else:
        o_ref[...] = jnp.zeros_like(o_ref)

    def run_chunk(base, m_rows):
        lane = lax.broadcasted_iota(I32, (tm, m_rows), 1)
        ptf = jnp.where(member & (pos - base == lane), 1.0, 0.0)
        pt = ptf.astype(BF16)
        p = ptf.T.astype(BF16)
        xc = _dot(p, xb).astype(BF16)
        cw = _dot(p, c1) + _dot(p, c2) + _dot(p, c3)
        y = jnp.zeros((m_rows, D_MODEL), F32)
        for j in range(EXPERTS_PER_GROUP):
            hg = _dot(xc, wg_ref[j])
            hu = _dot(xc, wu_ref[j])
            h = (hg * _sigmoid(hg)) * hu * cw[:, j:j + 1]
            y = y + _dot(h.astype(BF16), wd_ref[j])
        yh = y.astype(BF16)
        yl = (y - yh.astype(F32)).astype(BF16)
        o_ref[...] += _dot(pt, yh) + _dot(pt, yl)

    @pl.when(count <= MOE_CHUNK)
    def _():
        @pl.when(count > 0)
        def _():
            run_chunk(0, MOE_CHUNK)

    @pl.when(count > MOE_CHUNK)
    def _():
        def big_chunk(k, carry):
            run_chunk(k * 2 * MOE_CHUNK, 2 * MOE_CHUNK)
            return carry
        lax.fori_loop(0, (count + 2 * MOE_CHUNK - 1) // (2 * MOE_CHUNK), big_chunk, 0)


def _moe_single_tile(x, rt, ltri, wg, wu, wd, layer):
    n = x.shape[0]
    row = lambda c: pl.BlockSpec((n, c), lambda t, g: (0, 0))
    grp = lambda a, b: pl.BlockSpec((None, EXPERTS_PER_GROUP, a, b), lambda t, g: (layer, g, 0, 0))
    return pl.pallas_call(
        functools.partial(_moe_kernel, tm=n, group=None, accumulate=False),
        grid=(1, N_GROUPS),
        in_specs=[row(D_MODEL), row(LANE), pl.BlockSpec(ltri.shape, lambda t, g: (0, 0)),
                  grp(D_MODEL, D_EXPERT), grp(D_MODEL, D_EXPERT), grp(D_EXPERT, D_MODEL)],
        out_specs=row(D_MODEL),
        out_shape=jax.ShapeDtypeStruct((n, D_MODEL), F32),
        compiler_params=_params(2),
        name="grouped_moe_single_tile",
    )(x, rt, ltri, wg, wu, wd)


def _moe(x, rt, ltri, wg, wu, wd, layer, tm):
    n = x.shape[0]
    row = lambda c: pl.BlockSpec((tm, c), lambda t: (t, 0))
    acc = None
    for g in range(N_GROUPS):
        grp = lambda a, b: pl.BlockSpec((None, EXPERTS_PER_GROUP, a, b), lambda t, g=g: (layer, g, 0, 0),
                                        pipeline_mode=pl.Buffered(1))
        first = acc is None
        acc = pl.pallas_call(
            functools.partial(_moe_kernel, tm=tm, group=g, accumulate=not first),
            grid=(n // tm,),
            in_specs=[row(D_MODEL), row(LANE)] + ([] if first else [row(D_MODEL)]) + [
                _resident(ltri.shape), grp(D_MODEL, D_EXPERT), grp(D_MODEL, D_EXPERT), grp(D_EXPERT, D_MODEL)],
            out_specs=row(D_MODEL),
            out_shape=jax.ShapeDtypeStruct((n, D_MODEL), F32),
            input_output_aliases={} if first else {2: 0},
            compiler_params=_params(1),
            name="grouped_moe",
        )(*((x, rt) + (() if first else (acc,)) + (ltri, wg, wu, wd)))
    return acc


def _ln2_kernel(x_ref, y_ref, g_ref, b_ref, o_ref):
    o_ref[...] = _layer_norm(ALPHA * x_ref[...] + y_ref[...], g_ref[...], b_ref[...])


def _ln2(x, y, g, b, tm):
    n = x.shape[0]
    row = pl.BlockSpec((tm, D_MODEL), lambda i: (i, 0))
    vec = _resident((1, D_MODEL))
    return pl.pallas_call(
        _ln2_kernel, grid=(n // tm,), in_specs=[row, row, vec, vec], out_specs=row,
        out_shape=jax.ShapeDtypeStruct((n, D_MODEL), F32),
        compiler_params=_params(1), name="moe_residual_ln",
    )(x, y, g, b)


PAGES_PER_STEP = 32
ROWS_PER_STEP = PAGES_PER_STEP * PAGE_SIZE
BLOCKS_PER_STEP = ROWS_PER_STEP // BLOCK
ATTN_BLOCKS_PER_STEP = LANE
ATTN_ROWS_PER_STEP = ATTN_BLOCKS_PER_STEP * BLOCK
HALF = KV_HEADS * HEAD_DIM * 2
ROW_SLABS = KV_SLOTS * KV_HEADS
WIN_SLABS = 2 * KV_HEADS


def _page_specs(layer, n_pages):
    def spec(j):
        return pl.BlockSpec((None, None, PAGE_SIZE * ROW_SLABS, HEAD_DIM),
                            lambda b, s, pt: (layer, pt[b * n_pages + s * PAGES_PER_STEP + j], 0, 0))
    return [spec(j) for j in range(PAGES_PER_STEP)]


def _slab(page, c):
    return page[pl.ds(c, PAGE_SIZE, stride=ROW_SLABS), :]


def _sample_compress_kernel(pt_ref, *refs):
    pages = refs[:PAGES_PER_STEP]
    pe_ref, w1_ref, w2_ref, o_ref, ksel_ref, vsel_ref, acc_ref = refs[PAGES_PER_STEP:]
    for j, page in enumerate(pages):
        rows = slice(j * PAGE_SIZE, (j + 1) * PAGE_SIZE)
        for g in range(KV_HEADS):
            ksel_ref[g, rows, :] = _slab(page, 2 * KV_HEADS + g).astype(BF16)
            vsel_ref[g, rows, :] = _slab(page, 3 * KV_HEADS + g).astype(BF16)
    m_rows = BLOCKS_PER_STEP * ROW_SLABS
    acc = jnp.zeros((m_rows, 2 * HEAD_DIM), F32)
    for lp in range(BLOCK // 2):
        parts = []
        for l in (2 * lp, 2 * lp + 1):
            tiles = [page[(h * BLOCK + l) * ROW_SLABS:(h * BLOCK + l + 1) * ROW_SLABS, :]
                     for page in pages for h in range(PAGE_SIZE // BLOCK)]
            pe = jnp.concatenate([pe_ref[l]] * BLOCKS_PER_STEP, axis=0)
            parts.append(jnp.concatenate(tiles, axis=0) + pe)
        lhs = jnp.concatenate(parts, axis=1).astype(BF16)
        acc = acc + _dot(lhs, w1_ref[lp])
    for s in range(2):
        acc_ref[s] = acc[:, s * HEAD_DIM:(s + 1) * HEAD_DIM]
    for s in range(2):
        for g in range(KV_HEADS):
            c = s * KV_HEADS + g
            h = _gelu(acc_ref[s, pl.ds(c, BLOCKS_PER_STEP, stride=ROW_SLABS), :]).astype(BF16)
            o_ref[c] = _dot(h, w2_ref[s]).astype(BF16)


def _sample_compress(page_table, cache, layer, pe_rows, w1_cat, w2, db, n_pages):
    n_steps = n_pages // PAGES_PER_STEP
    nblk = n_pages * PAGE_SIZE // BLOCK
    const = lambda shape: pl.BlockSpec(shape, lambda b, s, pt: (0,) * len(shape))
    sel_spec = pl.BlockSpec((None, KV_HEADS, ROWS_PER_STEP, HEAD_DIM), lambda b, s, pt: (b, 0, s, 0))
    sel_shape = jax.ShapeDtypeStruct((db, KV_HEADS, n_pages * PAGE_SIZE, HEAD_DIM), BF16)
    return pl.pallas_call(
        _sample_compress_kernel,
        grid_spec=pltpu.PrefetchScalarGridSpec(
            num_scalar_prefetch=1, grid=(db, n_steps),
            in_specs=_page_specs(layer, n_pages) + [const(pe_rows.shape), const(w1_cat.shape), const(w2.shape)],
            out_specs=[pl.BlockSpec((None, 4, BLOCKS_PER_STEP, HEAD_DIM), lambda b, s, pt: (b, 0, s, 0)),
                       sel_spec, sel_spec],
            scratch_shapes=[pltpu.VMEM((2, BLOCKS_PER_STEP * ROW_SLABS, HEAD_DIM), F32)]),
        out_shape=[jax.ShapeDtypeStruct((db, 4, nblk, HEAD_DIM), BF16), sel_shape, sel_shape],
        compiler_params=_params(2),
        name="sample_compress",
    )(page_table, *([cache] * PAGES_PER_STEP), pe_rows, w1_cat, w2)


def _compress_operands(pe, w1):
    pe_rows = jnp.zeros((BLOCK, ROW_SLABS, HEAD_DIM), F32)
    for s in range(2):
        for g in range(KV_HEADS):
            pe_rows = pe_rows.at[:, s * KV_HEADS + g, :].set(pe[s])
    w1_cat = w1.transpose(1, 2, 0, 3).reshape(BLOCK // 2, 2 * HEAD_DIM, 2 * HEAD_DIM).astype(BF16)
    return pe_rows, w1_cat


def _softmax_update(s, v, m, l, acc):
    m_new = jnp.maximum(m, jnp.max(s, axis=1, keepdims=True))
    a = jnp.exp2(m - m_new)
    p = jnp.exp2(s - m_new)
    return m_new, a * l + jnp.sum(p, axis=1, keepdims=True), a * acc + _dot(p.astype(BF16), v)


def _sample_attn_kernel(k_ref, v_ref, q_ref, gates_ref, kvc_ref, new_ref, nwin_ref, nwin4_ref, cwin_ref, e_ref,
                        o_ref, wout_ref, sel_ref, ocmp_ref, m_ref, l_ref, acc_ref, *, ds, past_len, n_steps):
    step = pl.program_id(1)
    rows = Q_PER_KV * ds
    nc = past_len // BLOCK
    ns = -(-(past_len + ds) // BLOCK)
    ns_pad = -(-ns // LANE) * LANE
    qi = lax.broadcasted_iota(I32, (rows, 1), 0) % ds
    qpos = past_len + qi

    def q_group(g):
        return jnp.concatenate([q_ref[:, (g * Q_PER_KV + r) * HEAD_DIM:(g * Q_PER_KV + r + 1) * HEAD_DIM]
                                for r in range(Q_PER_KV)], axis=0).astype(BF16)

    @pl.when(step == 0)
    def _():
        blk_c = lax.broadcasted_iota(I32, (rows, nc), 1)
        cmask = (blk_c + 1) * BLOCK - 1 <= qpos
        blk = lax.broadcasted_iota(I32, (ds, ns_pad), 1)
        qpos_q = past_len + lax.broadcasted_iota(I32, (ds, 1), 0)
        cur = qpos_q // BLOCK
        forced = (blk == 0) | (blk == cur) | (blk == cur - 1)
        valid = (blk <= cur) & (blk < ns)
        for g in range(KV_HEADS):
            qg = q_group(g)
            sc = jnp.where(cmask, lax.dot_general(qg, kvc_ref[g], _NT, preferred_element_type=F32), NEG)
            m = jnp.max(sc, axis=1, keepdims=True)
            e = jnp.where(cmask, jnp.exp2(sc - m), 0.0)
            p = e / jnp.maximum(jnp.sum(e, axis=1, keepdims=True), 1e-30)
            ocmp_ref[g] = _dot(p.astype(BF16), kvc_ref[KV_HEADS + g])
            imp = p[0:ds]
            for r in range(1, Q_PER_KV):
                imp = imp + p[r * ds:(r + 1) * ds]
            imp = jnp.concatenate([imp, jnp.zeros((ds, ns_pad - nc), F32)], axis=1)
            score = jnp.where(valid, jnp.where(forced, FORCED_SCORE, imp), -1.0)
            rank = jnp.zeros((ds, ns_pad), I32)
            for j in range(ns):
                cj = score[:, j:j + 1]
                rank = rank + ((cj > score) | ((cj == score) & (blk > j))).astype(I32)
            sel = ((rank < min(N_SELECT, ns)) & valid).astype(F32)
            sel = jnp.concatenate([sel] * Q_PER_KV, axis=0)
            for t in range(n_steps):
                sel_ref[g, t] = sel[:, t * ATTN_BLOCKS_PER_STEP:(t + 1) * ATTN_BLOCKS_PER_STEP].astype(BF16)
            sel_ref[g, n_steps] = jnp.broadcast_to(sel[:, nc:nc + 1], (rows, LANE)).astype(BF16)
            m_ref[g] = jnp.full((rows, 1), NEG, F32)
            l_ref[g] = jnp.zeros((rows, 1), F32)
            acc_ref[g] = jnp.zeros((rows, HEAD_DIM), F32)

    for g in range(KV_HEADS):
        qg = q_group(g)
        hit = _dot(sel_ref[g, step], e_ref[...])
        s = lax.dot_general(qg, k_ref[g], _NT, preferred_element_type=F32)
        s = jnp.where(hit > 0.5, s, NEG)
        m, l, acc = _softmax_update(s, v_ref[g], m_ref[g], l_ref[g], acc_ref[g])
        m_ref[g] = m
        l_ref[g] = l
        acc_ref[g] = acc

    @pl.when(step == n_steps - 1)
    def _():
        pad = jnp.zeros((2 * SUBLANE - ds, HEAD_DIM), F32)
        col = lax.broadcasted_iota(I32, (rows, 2 * SUBLANE), 1)
        new_ok = (col <= qi) & (col < ds)
        wpos = lax.broadcasted_iota(I32, (rows, WINDOW), 1)
        for g in range(KV_HEADS):
            qg = q_group(g)
            kn = jnp.concatenate([new_ref[:, (2 * KV_HEADS + g) * HEAD_DIM:(2 * KV_HEADS + g + 1) * HEAD_DIM], pad], axis=0).astype(BF16)
            vn = jnp.concatenate([new_ref[:, (3 * KV_HEADS + g) * HEAD_DIM:(3 * KV_HEADS + g + 1) * HEAD_DIM], pad], axis=0).astype(BF16)
            s = lax.dot_general(qg, kn, _NT, preferred_element_type=F32)
            picked = sel_ref[g, n_steps][:, 0:2 * SUBLANE].astype(F32) > 0.5
            s = jnp.where(new_ok & picked, s, NEG)
            m, l, acc = _softmax_update(s, vn, m_ref[g], l_ref[g], acc_ref[g])
            o_sel = acc / l
            kw = cwin_ref[pl.ds(g, WINDOW, stride=WIN_SLABS), :].astype(BF16)
            vw = cwin_ref[pl.ds(KV_HEADS + g, WINDOW, stride=WIN_SLABS), :].astype(BF16)
            s = lax.dot_general(qg, kw, _NT, preferred_element_type=F32)
            s = jnp.where(wpos > qi, s, NEG)
            m, l, acc = _softmax_update(s, vw, jnp.full((rows, 1), NEG, F32), jnp.zeros((rows, 1), F32),
                                        jnp.zeros((rows, HEAD_DIM), F32))
            kn = jnp.concatenate([nwin_ref[:, g * HEAD_DIM:(g + 1) * HEAD_DIM], pad], axis=0).astype(BF16)
            vn = jnp.concatenate([nwin_ref[:, (KV_HEADS + g) * HEAD_DIM:(KV_HEADS + g + 1) * HEAD_DIM], pad], axis=0).astype(BF16)
            s = lax.dot_general(qg, kn, _NT, preferred_element_type=F32)
            s = jnp.where(new_ok, s, NEG)
            m, l, acc = _softmax_update(s, vn, m, l, acc)
            o_win = acc / l
            o_cmp = ocmp_ref[g]
            for r in range(Q_PER_KV):
                rs = slice(r * ds, (r + 1) * ds)
                c0 = g * LANE
                g_cmp = gates_ref[:, c0 + r:c0 + r + 1]
                g_sel = gates_ref[:, c0 + Q_PER_KV + r:c0 + Q_PER_KV + r + 1]
                g_win = gates_ref[:, c0 + 2 * Q_PER_KV + r:c0 + 2 * Q_PER_KV + r + 1]
                h = g * Q_PER_KV + r
                o_ref[:, h * HEAD_DIM:(h + 1) * HEAD_DIM] = g_cmp * o_cmp[rs] + g_sel * o_sel[rs] + g_win * o_win[rs]
        wout_ref[0:(WINDOW - ds) * WIN_SLABS, :] = cwin_ref[ds * WIN_SLABS:WINDOW * WIN_SLABS, :]
        wout_ref[(WINDOW - ds) * WIN_SLABS:WINDOW * WIN_SLABS, :] = nwin4_ref[...]


def _sample_attention(ksel, vsel, layer, q, gates, kvc, kv4, kwin, kwin4, cache_win, expand, db, ds):
    past_len = ksel.shape[2]
    n_steps = past_len // ATTN_ROWS_PER_STEP
    assert ds == SUBLANE and past_len % ATTN_ROWS_PER_STEP == 0 and past_len >= WINDOW
    assert cache_win.shape[2] == WINDOW * WIN_SLABS
    rows = Q_PER_KV * ds
    per_b = lambda c: pl.BlockSpec((ds, c), lambda b, s: (b, 0))
    sel_spec = pl.BlockSpec((None, KV_HEADS, ATTN_ROWS_PER_STEP, HEAD_DIM), lambda b, s: (b, 0, s, 0))
    return pl.pallas_call(
        functools.partial(_sample_attn_kernel, ds=ds, past_len=past_len, n_steps=n_steps),
        grid=(db, n_steps),
        in_specs=[sel_spec, sel_spec,
                  per_b(D_ATTN), per_b(KV_HEADS * LANE),
                  pl.BlockSpec((None, 4, past_len // BLOCK, HEAD_DIM), lambda b, s: (b, 0, 0, 0)),
                  per_b(2 * HALF), per_b(HALF),
                  pl.BlockSpec((ds * WIN_SLABS, HEAD_DIM), lambda b, s: (b, 0)),
                  pl.BlockSpec((None, None, WINDOW * WIN_SLABS, HEAD_DIM), lambda b, s: (layer, b, 0, 0)),
                  pl.BlockSpec(expand.shape, lambda b, s: (0, 0))],
        out_specs=[per_b(D_ATTN),
                   pl.BlockSpec((None, WINDOW * WIN_SLABS, HEAD_DIM), lambda b, s: (b, 0, 0))],
        scratch_shapes=[pltpu.VMEM((KV_HEADS, n_steps + 1, rows, LANE), BF16),
                        pltpu.VMEM((KV_HEADS, rows, HEAD_DIM), F32),
                        pltpu.VMEM((KV_HEADS, rows, 1), F32),
                        pltpu.VMEM((KV_HEADS, rows, 1), F32),
                        pltpu.VMEM((KV_HEADS, rows, HEAD_DIM), F32)],
        out_shape=[jax.ShapeDtypeStruct((db * ds, D_ATTN), F32),
                   jax.ShapeDtypeStruct((db, WINDOW * WIN_SLABS, HEAD_DIM), F32)],
        compiler_params=_params(2),
        name="sample_attention",
    )(ksel, vsel, q, gates, kvc, kv4, kwin, kwin4, cache_win, expand)


def _gate_columns(w_gates):
    d = w_gates.shape[0]
    w = w_gates.reshape(d, 3, KV_HEADS, Q_PER_KV).transpose(0, 2, 1, 3).reshape(d, KV_HEADS, 3 * Q_PER_KV)
    return jnp.pad(w, ((0, 0), (0, 0), (0, LANE - 3 * Q_PER_KV))).reshape(d, KV_HEADS * LANE)


def kernel(x_prompt, x_sample, cache_kv, cache_win, state_conv, state_h, page_table, w_in, cmp_pe, cmp_w1, cmp_w2, conv_w, conv_b, lru_w_a, lru_b_a, lru_w_i, lru_b_i, lru_lam, group_norm, w_out, ln_g, ln_b, router_w, router_b, exp_w_gate, exp_w_up, exp_w_down):
    b, t, _ = x_prompt.shape
    db, ds, _ = x_sample.shape
    past_len = page_table.shape[1] * PAGE_SIZE
    n_p, n_s = b * t, db * ds
    tm_p, tm_moe, tq, tk, tt = 256, 512, 512, 1024, 256
    n_s_pad = -(-n_s // LANE) * LANE

    xp = x_prompt.reshape(n_p, D_MODEL)
    xs = x_sample.reshape(n_s, D_MODEL)
    expand = (jnp.arange(t)[None, :] // BLOCK == jnp.arange(t // BLOCK)[:, None]).astype(BF16)
    ltri = lambda m: (jnp.arange(m)[None, :] < jnp.arange(m)[:, None]).astype(BF16)
    ltri_p, ltri_s = ltri(tm_moe), ltri(n_s_pad)
    rw = jnp.pad(router_w, ((0, 0), (0, LANE - N_EXPERTS))).astype(BF16)
    rb = jnp.pad(router_b, (0, LANE - N_EXPERTS)).reshape(1, LANE)
    conv0_p = jnp.zeros((b, SUBLANE, D_RNN), F32)
    h0_p = jnp.zeros((b, SUBLANE, D_RNN), F32)
    pad_rows = SUBLANE - (CONV_WIDTH - 1)
    n_pages = page_table.shape[1]
    pt_flat = page_table.reshape(-1)
    cache = cache_kv.reshape(DEPTH, cache_kv.shape[1], PAGE_SIZE * ROW_SLABS, HEAD_DIM)
    cwin = cache_win.reshape(DEPTH, db, cache_win.shape[2] * WIN_SLABS, HEAD_DIM)
    expand_s = (jnp.arange(ATTN_ROWS_PER_STEP)[None, :] // BLOCK
                == jnp.arange(ATTN_BLOCKS_PER_STEP)[:, None]).astype(BF16)

    wo = w_out.astype(BF16)
    wg, wu, wd = exp_w_gate.astype(BF16), exp_w_up.astype(BF16), exp_w_down.astype(BF16)

    outs = [[] for _ in range(8)]
    for l in range(DEPTH):
        c_gate = 1024 + 6 * KV_HEADS * HEAD_DIM
        w_main = jnp.concatenate([w_in[l, :, :c_gate], w_in[l, :, c_gate + 3 * N_HEADS:]], axis=1).astype(BF16)
        w_gate = _gate_columns(w_in[l, :, c_gate:c_gate + 3 * N_HEADS]).astype(BF16)
        w1 = cmp_w1[l].astype(BF16)
        w2 = cmp_w2[l].astype(BF16)
        w_ai = jnp.concatenate([lru_w_a[l], lru_w_i[l]], axis=-1).astype(BF16)
        vec = lambda v: v.reshape(1, -1)
        lru = (conv_w[l], vec(conv_b[l]), w_ai, vec(lru_b_a[l]), vec(lru_b_i[l]), vec(lru_lam[l]))
        gn = vec(group_norm[l])

        q, kv4, kwin, kvb, gates, u, gg = _in_proj(xp, w_main, w_gate, tm_p)
        kvc = _compress(kv4, cmp_pe[l], w1, w2, b, t)
        attn = _prompt_attention(q, gates, kvc, kvb, expand, b, t, tq, tk)
        rnn, hl = _rglru(u, gg, conv0_p, h0_p, *lru, b, t, tt)
        x1, x1b, rt = _mixer(attn, rnn, xp, gn, wo, l, vec(ln_g[l, 0]), vec(ln_b[l, 0]), rw, rb, tm_p)
        y = _moe(x1b, rt, ltri_p, wg, wu, wd, l, tm_moe)
        xp = _ln2(x1, y, vec(ln_g[l, 1]), vec(ln_b[l, 1]), tm_p)
        outs[0].append(kv4.reshape(b, t, KV_SLOTS, KV_HEADS, HEAD_DIM))
        w_keep = min(WINDOW, t)
        outs[2].append(kwin.reshape(b, t, HALF)[:, -w_keep:].reshape(b, w_keep, 2, KV_HEADS, HEAD_DIM))
        outs[4].append(u.reshape(b, t, D_RNN)[:, -(CONV_WIDTH - 1):])
        outs[6].append(hl.reshape(b, D_RNN))

        q, kv4, kwin, kvb, gates, u, gg = _in_proj(xs, w_main, w_gate, n_s)
        kvc, ksel, vsel = _sample_compress(pt_flat, cache, l, *_compress_operands(cmp_pe[l], cmp_w1[l]), w2,
                                           db, n_pages)
        attn, wnew = _sample_attention(ksel, vsel, l, q.astype(F32), gates, kvc, kv4, kwin,
                                       kwin.reshape(n_s * WIN_SLABS, HEAD_DIM), cwin, expand_s, db, ds)
        conv0_s = jnp.pad(state_conv[l], ((0, 0), (pad_rows, 0), (0, 0)))
        h0_s = jnp.pad(state_h[l][:, None, :], ((0, 0), (0, SUBLANE - 1), (0, 0)))
        rnn, hl = _rglru(u, gg, conv0_s, h0_s, *lru, db, ds, ds)
        x1, x1b, rt = _mixer(attn, rnn, xs, gn, wo, l, vec(ln_g[l, 0]), vec(ln_b[l, 0]),
                             rw, rb, n_s)
        x1p = jnp.pad(x1b, ((0, n_s_pad - n_s), (0, 0)))
        rtp = jnp.pad(rt, ((0, n_s_pad - n_s), (0, 0)), constant_values=-1.0)
        y = _moe_single_tile(x1p, rtp, ltri_s, wg, wu, wd, l)[:n_s]
        xs = _ln2(x1, y, vec(ln_g[l, 1]), vec(ln_b[l, 1]), n_s)
        outs[1].append(kv4.reshape(db, ds, KV_SLOTS, KV_HEADS, HEAD_DIM))
        outs[3].append(wnew.reshape(db, WINDOW, 2, KV_HEADS, HEAD_DIM))
        xpc = jnp.concatenate([state_conv[l], u.reshape(db, ds, D_RNN)], axis=1)
        outs[5].append(xpc[:, -(CONV_WIDTH - 1):])
        outs[7].append(hl.reshape(db, D_RNN))

    st = [jnp.stack(o) for o in outs]
    return (xp.reshape(b, t, D_MODEL), xs.reshape(db, ds, D_MODEL),
            st[0], st[1], st[2], st[3], st[4], st[5], st[6], st[7])
```
